```python
import math
import jax, jax.numpy as jnp
from jax import lax
import numpy as np

D_MODEL = 1024
BATCH = 8
SEQ = 8192
DEPTH = 2
DEC_BATCH = 16
DEC_SEQ = 32
PAST_LEN = 4096

CHUNK = 64
SSM_HEADS = 16
SSM_HEAD_DIM = 64
D_SSM = SSM_HEADS * SSM_HEAD_DIM
SSM_GROUPS = 2
D_STATE = 128
CONV_W = 4
SSD_CHUNK = 64
D_CONV = D_SSM + 2 * SSM_GROUPS * D_STATE
ATT_HEADS = 16
ATT_HEAD_DIM = 64
D_ATT = ATT_HEADS * ATT_HEAD_DIM
BAND_PREV_CHUNKS = 8
BAND_PAST = BAND_PREV_CHUNKS * CHUNK
REL_CLIP = 256
D_MIX = D_SSM + D_ATT
D_IN_PROJ = D_SSM + D_CONV + SSM_HEADS + 3 * D_ATT
N_EXPERTS = 16
N_EXPERT_GROUPS = 4
EXPERTS_PER_GROUP = N_EXPERTS // N_EXPERT_GROUPS
TOP_K = 2
D_EXPERT = 1024
ALPHA = (2 * DEPTH) ** 0.25
BETA = (8 * DEPTH) ** -0.25
NORM_EPS = 1e-5

kernel_name = "hybrid_ssd_chunkband_moe_deepnorm_stream_step"


def layer_norm(x, g, b):
    xf = x.astype(jnp.float32)
    mu = xf.mean(-1, keepdims=True)
    var = jnp.square(xf - mu).mean(-1, keepdims=True)
    return ((xf - mu) * lax.rsqrt(var + NORM_EPS) * g.astype(jnp.float32) + b.astype(jnp.float32)).astype(x.dtype)


def rms_norm(x, g):
    xf = x.astype(jnp.float32)
    return xf * lax.rsqrt(jnp.mean(xf * xf, -1, keepdims=True) + NORM_EPS) * g.astype(jnp.float32)


def causal_conv(xbc, conv_prev, w, b):
    T = xbc.shape[1]
    full = jnp.concatenate([conv_prev.astype(xbc.dtype), xbc], axis=1)
    y = sum(full[:, i:i + T] * w[i] for i in range(CONV_W)) + b
    return jax.nn.silu(y), full[:, full.shape[1] - (CONV_W - 1):]


def ssd_scan(x, dt, a, bm, cm, h0):
    b, T, H, P = x.shape
    G, R, N = SSM_GROUPS, H // SSM_GROUPS, D_STATE
    lc = min(SSD_CHUNK, T)
    nc = T // lc
    x = x.reshape(b, nc, lc, G, R, P)
    dt = dt.reshape(b, nc, lc, G, R)
    bm = bm.reshape(b, nc, lc, G, N)
    cm = cm.reshape(b, nc, lc, G, N)
    a_cs = jnp.cumsum(dt * a.reshape(G, R), axis=2)
    seg = a_cs[:, :, :, None] - a_cs[:, :, None, :]
    causal = jnp.tril(jnp.ones((lc, lc), dtype=bool))[:, :, None, None]
    decay = jnp.exp(jnp.where(causal, seg, -jnp.inf))
    cb = jnp.einsum('bclgn,bcsgn->bclsg', cm, bm).astype(jnp.float32)
    m = cb[..., None] * decay * dt[:, :, None]
    y_diag = jnp.einsum('bclsgr,bcsgrp->bclgrp', m, x)
    decay_end = jnp.exp(a_cs[:, :, -1:] - a_cs)
    states = jnp.einsum('bclgn,bclgrp->bcgrpn', bm, (decay_end * dt)[..., None] * x)
    chunk_decay = jnp.exp(a_cs[:, :, -1])

    def step(h, inp):
        s_c, d_c = inp
        return h * d_c[..., None, None] + s_c, h

    h_init = h0.reshape(b, G, R, P, N).astype(jnp.float32)
    h_final, h_prev = lax.scan(step, h_init, (jnp.moveaxis(states, 1, 0), jnp.moveaxis(chunk_decay, 1, 0)))
    h_prev = jnp.moveaxis(h_prev, 0, 1)
    y_off = jnp.einsum('bclgn,bcgrpn->bclgrp', cm, h_prev) * jnp.exp(a_cs)[..., None]
    y = (y_diag + y_off).reshape(b, T, H, P)
    return y, h_final.reshape(b, H, P, N)


def rel_bias_lookup(table, rel):
    idx = jnp.clip(rel, -REL_CLIP, REL_CLIP) + REL_CLIP
    return jnp.transpose(table[idx], (2, 0, 1)).astype(jnp.float32)


def attend(q, k, v, bias, valid=None):
    s = jnp.einsum('blhd,bshd->bhls', q, k).astype(jnp.float32) * (ATT_HEAD_DIM ** -0.5) + bias
    if valid is not None:
        s = jnp.where(valid, s, -1e30)
    p = jax.nn.softmax(s, axis=-1)
    return jnp.einsum('bhls,bshd->blhd', p.astype(v.dtype), v)


def band_attention_prompt(q, k, v, rel_tab):
    b, T, H, Dh = q.shape
    nc = T // CHUNK
    band = BAND_PAST + CHUNK
    pad = jnp.zeros((b, BAND_PAST, H, Dh), k.dtype)
    kp = jnp.concatenate([pad, k], axis=1)
    vp = jnp.concatenate([pad.astype(v.dtype), v], axis=1)
    s_idx = jnp.arange(band)
    rel = BAND_PAST + jnp.arange(CHUNK)[:, None] - s_idx[None, :]
    bias = rel_bias_lookup(rel_tab, rel)

    def one_chunk(c):
        qc = lax.dynamic_slice_in_dim(q, c * CHUNK, CHUNK, axis=1)
        kc = lax.dynamic_slice_in_dim(kp, c * CHUNK, band, axis=1)
        vc = lax.dynamic_slice_in_dim(vp, c * CHUNK, band, axis=1)
        valid = (c * CHUNK + s_idx) >= BAND_PAST
        return attend(qc, kc, vc, bias, valid)

    out = lax.map(one_chunk, jnp.arange(nc))
    return jnp.moveaxis(out, 0, 1).reshape(b, T, H, Dh)


def mixer(x, p, i, conv_prev, h0, k_prev, v_prev):
    b, T, _ = x.shape
    proj = x @ p['w_in'][i]
    z, xbc, dt_raw, q, k, v = jnp.split(
        proj, np.cumsum([D_SSM, D_CONV, SSM_HEADS, D_ATT, D_ATT]).tolist(), axis=-1)
    xbc, new_conv = causal_conv(xbc, conv_prev, p['conv_w'][i], p['conv_b'][i])
    xs, bm, cm = jnp.split(xbc, [D_SSM, D_SSM + SSM_GROUPS * D_STATE], axis=-1)
    xs = xs.reshape(b, T, SSM_HEADS, SSM_HEAD_DIM)
    bm = bm.reshape(b, T, SSM_GROUPS, D_STATE)
    cm = cm.reshape(b, T, SSM_GROUPS, D_STATE)
    dt = jax.nn.softplus(dt_raw.astype(jnp.float32) + p['dt_bias'][i].astype(jnp.float32))
    a = -jnp.exp(p['a_log'][i].astype(jnp.float32))
    y, h_new = ssd_scan(xs, dt, a, bm, cm, h0)
    y = y + p['d_skip'][i].astype(jnp.float32)[:, None] * xs
    y_ssd = rms_norm(y.reshape(b, T, D_SSM) * jax.nn.silu(z.astype(jnp.float32)), p['ssd_norm_g'][i])
    q = q.reshape(b, T, ATT_HEADS, ATT_HEAD_DIM)
    k = k.reshape(b, T, ATT_HEADS, ATT_HEAD_DIM)
    v = v.reshape(b, T, ATT_HEADS, ATT_HEAD_DIM)
    if k_prev is None:
        o = band_attention_prompt(q, k, v, p['rel_bias'][i])
        keep = min(BAND_PAST, T)
        k_state, v_state = k[:, T - keep:], v[:, T - keep:]
    else:
        past = k_prev.shape[1]
        k_all = jnp.concatenate([k_prev.astype(k.dtype), k], axis=1)
        v_all = jnp.concatenate([v_prev.astype(v.dtype), v], axis=1)
        rel = past + jnp.arange(T)[:, None] - jnp.arange(past + T)[None, :]
        o = attend(q, k_all, v_all, rel_bias_lookup(p['rel_bias'][i], rel))
        k_state, v_state = k, v
    y_att = rms_norm(o.reshape(b, T, D_ATT), p['attn_norm_g'][i])
    out = jnp.concatenate([y_ssd, y_att], axis=-1).astype(x.dtype) @ p['w_out'][i]
    return out, (k_state, v_state, h_new, new_conv)


def moe(h, router_w, router_bias, w_gate, w_up, w_down):
    shp = h.shape
    t = h.reshape(-1, shp[-1])
    scores = jax.nn.softmax((t @ router_w).astype(jnp.float32), axis=-1)
    sel = (scores + router_bias.astype(jnp.float32)).reshape(-1, N_EXPERT_GROUPS, EXPERTS_PER_GROUP)
    group_score = lax.top_k(sel, TOP_K)[0].sum(-1)
    gidx = jnp.argmax(group_score, axis=-1)
    in_group = jnp.take_along_axis(sel, gidx[:, None, None], axis=1)[:, 0]
    _, local = lax.top_k(in_group, TOP_K)
    eidx = gidx[:, None] * EXPERTS_PER_GROUP + local
    gate = jnp.take_along_axis(scores, eidx, axis=-1)
    gate = gate / gate.sum(-1, keepdims=True)
    combine = jnp.einsum('nk,nke->ne', gate, jax.nn.one_hot(eidx, N_EXPERTS, dtype=jnp.float32))
    out = jnp.zeros(t.shape, jnp.float32)
    for e in range(N_EXPERTS):
        he = jax.nn.silu(t @ w_gate[e]) * (t @ w_up[e])
        out = out + combine[:, e:e + 1] * (he @ w_down[e])
    return out.astype(h.dtype).reshape(shp)


def trunk(x, p, conv_prev, h0, k_prev, v_prev):
    ks, vs, hs, cs = [], [], [], []
    for i in range(DEPTH):
        mix, (k_s, v_s, h_s, c_s) = mixer(
            x, p, i, conv_prev[i], h0[i],
            None if k_prev is None else k_prev[i], None if v_prev is None else v_prev[i])
        h = layer_norm(ALPHA * x + mix, p['ln1_g'][i], p['ln1_b'][i])
        f = moe(h, p['router_w'], p['router_bias'], p['w_gate'][i], p['w_up'][i], p['w_down'][i])
        x = layer_norm(ALPHA * h + f, p['ln2_g'][i], p['ln2_b'][i])
        ks.append(k_s); vs.append(v_s); hs.append(h_s); cs.append(c_s)
    return x, jnp.stack(ks), jnp.stack(vs), jnp.stack(hs), jnp.stack(cs)


def setup_inputs(seed: int = 0) -> dict:
    key = jax.random.key(seed)
    ks = jax.random.split(key, 32)
    f32 = jnp.float32
    att_past = min(BAND_PAST, PAST_LEN)
    nrm = lambda k, shp, s: jax.random.normal(k, shp, f32) * s
    dt0 = jnp.exp(jax.random.uniform(ks[10], (DEPTH, SSM_HEADS), f32) * (math.log(0.1) - math.log(0.001)) + math.log(0.001))
    return {
        'x_prompt': nrm(ks[0], (BATCH, SEQ, D_MODEL), 1.0),
        'x_sample': nrm(ks[1], (DEC_BATCH, DEC_SEQ, D_MODEL), 1.0),
        'cache_k': nrm(ks[2], (DEPTH, DEC_BATCH, att_past, ATT_HEADS, ATT_HEAD_DIM), 1.0),
        'cache_v': nrm(ks[3], (DEPTH, DEC_BATCH, att_past, ATT_HEADS, ATT_HEAD_DIM), 1.0),
        'state_ssm': nrm(ks[4], (DEPTH, DEC_BATCH, SSM_HEADS, SSM_HEAD_DIM, D_STATE), 0.5),
        'state_conv': nrm(ks[5], (DEPTH, DEC_BATCH, CONV_W - 1, D_CONV), 1.0),
        'w_in': nrm(ks[6], (DEPTH, D_MODEL, D_IN_PROJ), D_MODEL ** -0.5),
        'conv_w': nrm(ks[7], (DEPTH, CONV_W, D_CONV), CONV_W ** -0.5),
        'conv_b': nrm(ks[8], (DEPTH, D_CONV), 0.01),
        'dt_bias': dt0 + jnp.log(-jnp.expm1(-dt0)),
        'a_log': jnp.log(jax.random.uniform(ks[11], (DEPTH, SSM_HEADS), f32, 1.0, 16.0)),
        'd_skip': 1.0 + nrm(ks[12], (DEPTH, SSM_HEADS), 0.1),
        'ssd_norm_g': 1.0 + nrm(ks[13], (DEPTH, D_SSM), 0.02),
        'attn_norm_g': 1.0 + nrm(ks[14], (DEPTH, D_ATT), 0.02),
        'rel_bias': nrm(ks[15], (DEPTH, 2 * REL_CLIP + 1, ATT_HEADS), 0.1),
        'w_out': nrm(ks[16], (DEPTH, D_MIX, D_MODEL), BETA * D_MIX ** -0.5),
        'ln1_g': 1.0 + nrm(ks[17], (DEPTH, D_MODEL), 0.02),
        'ln1_b': nrm(ks[18], (DEPTH, D_MODEL), 0.02),
        'router_w': nrm(ks[19], (D_MODEL, N_EXPERTS), D_MODEL ** -0.5),
        'router_bias': nrm(ks[20], (N_EXPERTS,), 0.01),
        'w_gate': nrm(ks[21], (DEPTH, N_EXPERTS, D_MODEL, D_EXPERT), D_MODEL ** -0.5),
        'w_up': nrm(ks[22], (DEPTH, N_EXPERTS, D_MODEL, D_EXPERT), D_MODEL ** -0.5),
        'w_down': nrm(ks[23], (DEPTH, N_EXPERTS, D_EXPERT, D_MODEL), BETA * D_EXPERT ** -0.5),
        'ln2_g': 1.0 + nrm(ks[24], (DEPTH, D_MODEL), 0.02),
        'ln2_b': nrm(ks[25], (DEPTH, D_MODEL), 0.02),
    }


def reference(x_prompt, x_sample, cache_k, cache_v, state_ssm, state_conv, w_in, conv_w, conv_b,
              dt_bias, a_log, d_skip, ssd_norm_g, attn_norm_g, rel_bias, w_out, ln1_g, ln1_b,
              router_w, router_bias, w_gate, w_up, w_down, ln2_g, ln2_b):
    p = {'w_in': w_in, 'conv_w': conv_w, 'conv_b': conv_b, 'dt_bias': dt_bias, 'a_log': a_log,
         'd_skip': d_skip, 'ssd_norm_g': ssd_norm_g, 'attn_norm_g': attn_norm_g, 'rel_bias': rel_bias,
         'w_out': w_out, 'ln1_g': ln1_g, 'ln1_b': ln1_b, 'router_w': router_w, 'router_bias': router_bias,
         'w_gate': w_gate, 'w_up': w_up, 'w_down': w_down, 'ln2_g': ln2_g, 'ln2_b': ln2_b}
    bp = x_prompt.shape[0]
    conv0 = jnp.zeros((DEPTH, bp, CONV_W - 1, D_CONV), x_prompt.dtype)
    h00 = jnp.zeros((DEPTH, bp, SSM_HEADS, SSM_HEAD_DIM, D_STATE), jnp.float32)
    y_prompt, k_prompt, v_prompt, ssm_prompt, conv_prompt = trunk(x_prompt, p, conv0, h00, None, None)
    y_sample, k_sample, v_sample, ssm_sample, conv_sample = trunk(
        x_sample, p, state_conv, state_ssm, cache_k, cache_v)
    return (y_prompt, y_sample, k_prompt, v_prompt, ssm_prompt, conv_prompt,
            k_sample, v_sample, ssm_sample, conv_sample)
```

```python
import functools

import jax
import jax.numpy as jnp
from jax import lax
from jax.experimental import pallas as pl
from jax.experimental.pallas import tpu as pltpu

F32 = jnp.float32
BF16 = jnp.bfloat16

CHUNK = 64
BAND_PREV_CHUNKS = 8
BAND_PAST = BAND_PREV_CHUNKS * CHUNK
REL_CLIP = 256
SSM_GROUPS = 2
CONV_W = 4
N_EXPERT_GROUPS = 4
NORM_EPS = 1e-5
NEG_BIG = -1e30

LANES = 128
SUBLANES = 8
VMEM_LIMIT = 56 * 1024 * 1024

HIGHEST = lax.Precision.HIGHEST


def _cparams(sem):
    return pltpu.CompilerParams(dimension_semantics=sem, vmem_limit_bytes=VMEM_LIMIT)


def _silu(v):
    return v * jax.nn.sigmoid(v)


def _split3(v):
    p1 = v.astype(BF16)
    r1 = v - p1.astype(F32)
    p2 = r1.astype(BF16)
    p3 = (r1 - p2.astype(F32)).astype(BF16)
    return p1, p2, p3


def _full(shape):
    return pl.BlockSpec(shape, lambda *_: (0,) * len(shape))


def _in_proj_kernel(x_ref, wz_ref, wxbc_ref, wdt_ref, wq_ref, wk_ref, wv_ref,
                    z_ref, xbc_ref, dt_ref, q_ref, k_ref, v_ref):
    xb = x_ref[...].astype(BF16)
    z_ref[...] = jnp.dot(xb, wz_ref[...], preferred_element_type=F32)
    xbc_ref[...] = jnp.dot(xb, wxbc_ref[...], preferred_element_type=F32)
    dt_ref[...] = jnp.dot(xb, wdt_ref[...], preferred_element_type=F32)
    q_ref[...] = jnp.dot(xb, wq_ref[...], preferred_element_type=F32).astype(BF16)
    k_ref[...] = jnp.dot(xb, wk_ref[...], preferred_element_type=F32).astype(BF16)
    v_ref[...] = jnp.dot(xb, wv_ref[...], preferred_element_type=F32).astype(BF16)


def _in_proj(x2d, w, tm):
    n, d = x2d.shape
    d_ssm, d_conv, d_att = w['wz'].shape[1], w['wxbc'].shape[1], w['wq'].shape[1]
    row = lambda width: pl.BlockSpec((tm, width), lambda i: (i, 0))
    return pl.pallas_call(
        _in_proj_kernel,
        grid=(n // tm,),
        in_specs=[row(d), _full(w['wz'].shape), _full(w['wxbc'].shape), _full(w['wdt'].shape),
                  _full(w['wq'].shape), _full(w['wk'].shape), _full(w['wv'].shape)],
        out_specs=[row(d_ssm), row(d_conv), row(LANES), row(d_att), row(d_att), row(d_att)],
        out_shape=[jax.ShapeDtypeStruct((n, d_ssm), F32), jax.ShapeDtypeStruct((n, d_conv), F32),
                   jax.ShapeDtypeStruct((n, LANES), F32), jax.ShapeDtypeStruct((n, d_att), BF16),
                   jax.ShapeDtypeStruct((n, d_att), BF16), jax.ShapeDtypeStruct((n, d_att), BF16)],
        compiler_params=_cparams(("parallel",)),
        name="in_proj",
    )(x2d, w['wz'], w['wxbc'], w['wdt'], w['wq'], w['wk'], w['wv'])


def _ssd_kernel(xbc_ref, dt_ref, cprev_ref, h0_ref, convw_ref, convb_ref, dtb_ref, alog_ref,
                dskip_ref, tril_ref, ex_ref, y_ref, hout_ref, cbuf, hst, *, blk, d_ssm, d_state):
    c = pl.program_id(1)
    gw = d_ssm // SSM_GROUPS

    @pl.when(c == 0)
    def _():
        hst[...] = h0_ref[0]
        cbuf[0:SUBLANES, :] = cprev_ref[0]

    cbuf[SUBLANES:SUBLANES + blk, :] = xbc_ref[0]
    acc = jnp.broadcast_to(convb_ref[...], (blk, convb_ref.shape[1]))
    for i in range(CONV_W):
        lo = SUBLANES - (CONV_W - 1) + i
        acc = acc + cbuf[lo:lo + blk, :] * convw_ref[i:i + 1, :]
    cbuf[0:SUBLANES, :] = cbuf[blk:blk + SUBLANES, :]
    xc = _silu(acc)
    xs = xc[:, :d_ssm]
    bms = [xc[:, d_ssm + g * d_state: d_ssm + (g + 1) * d_state].astype(BF16) for g in range(SSM_GROUPS)]
    cms = [xc[:, d_ssm + (SSM_GROUPS + g) * d_state: d_ssm + (SSM_GROUPS + g + 1) * d_state].astype(BF16)
           for g in range(SSM_GROUPS)]

    dtr = dt_ref[0] + dtb_ref[...]
    dt = jnp.maximum(dtr, 0.0) + jnp.log1p(jnp.exp(-jnp.abs(dtr)))
    a = -jnp.exp(alog_ref[...])
    acs = sum(jnp.dot(tril_ref[...], piece, preferred_element_type=F32) for piece in _split3(dt * a))
    acs_t = acs.T
    acs_e = sum(jnp.dot(piece, ex_ref[...], preferred_element_type=F32) for piece in _split3(acs))
    dt_e = sum(jnp.dot(piece, ex_ref[...], preferred_element_type=F32) for piece in _split3(dt))
    last_e = acs_e[blk - 1:blk, :]
    dtx = dt_e * xs
    xw = (jnp.exp(last_e - acs_e) * dtx).astype(BF16)
    dtx = dtx.astype(BF16)
    eacs_e = jnp.exp(acs_e)
    cdec_e = jnp.exp(last_e)

    row = lax.broadcasted_iota(jnp.int32, (blk, blk), 0)
    col = lax.broadcasted_iota(jnp.int32, (blk, blk), 1)
    causal = row >= col
    lane = lax.broadcasted_iota(jnp.int32, (blk, LANES), 1)
    first_head = lane < 64

    pairs_per_group = gw // LANES
    for g in range(SSM_GROUPS):
        hg = hst[g]
        cb = lax.dot_general(cms[g], bms[g], (((1,), (1,)), ((), ())), preferred_element_type=F32)
        y_off = jnp.dot(cms[g], hg.astype(BF16), preferred_element_type=F32)
        for jp in range(pairs_per_group):
            p0 = g * gw + jp * LANES
            h0 = p0 // 64
            xpair = dtx[:, p0:p0 + LANES]
            outs = []
            for hh in (h0, h0 + 1):
                seg = acs[:, hh:hh + 1] - acs_t[hh:hh + 1, :]
                dec = jnp.exp(jnp.where(causal, seg, NEG_BIG))
                outs.append(jnp.dot((cb * dec).astype(BF16), xpair, preferred_element_type=F32))
            y_diag = jnp.where(first_head, outs[0], outs[1])
            sl = slice(p0, p0 + LANES)
            y_ref[0, :, sl] = (y_diag + y_off[:, jp * LANES:(jp + 1) * LANES] * eacs_e[:, sl]
                               + dskip_ref[:, sl] * xs[:, sl])
        st = lax.dot_general(bms[g], xw[:, g * gw:(g + 1) * gw], (((0,), (0,)), ((), ())),
                             preferred_element_type=F32)
        hst[g] = hg * cdec_e[:, g * gw:(g + 1) * gw] + st

    @pl.when(c == pl.num_programs(1) - 1)
    def _():
        hout_ref[0] = hst[...]


def _ssd(xbc, dt, cprev8, h0t, p, blk):
    b, t, d_conv = xbc.shape
    d_ssm = p['dskip_e'].shape[1]
    d_state = (d_conv - d_ssm) // (2 * SSM_GROUPS)
    gw = d_ssm // SSM_GROUPS
    tril = jnp.tril(jnp.ones((blk, blk), BF16))
    kern = functools.partial(_ssd_kernel, blk=blk, d_ssm=d_ssm, d_state=d_state)
    return pl.pallas_call(
        kern,
        grid=(b, t // blk),
        in_specs=[pl.BlockSpec((1, blk, d_conv), lambda i, c: (i, c, 0)),
                  pl.BlockSpec((1, blk, LANES), lambda i, c: (i, c, 0)),
                  pl.BlockSpec((1, SUBLANES, d_conv), lambda i, c: (i, 0, 0)),
                  pl.BlockSpec((1, SSM_GROUPS, d_state, gw), lambda i, c: (i, 0, 0, 0)),
                  _full(p['conv_w'].shape), _full(p['conv_b'].shape), _full(p['dt_bias'].shape),
                  _full(p['a_log'].shape), _full(p['dskip_e'].shape), _full(tril.shape),
                  _full(p['expand'].shape)],
        out_specs=[pl.BlockSpec((1, blk, d_ssm), lambda i, c: (i, c, 0)),
                   pl.BlockSpec((1, SSM_GROUPS, d_state, gw), lambda i, c: (i, 0, 0, 0))],
        out_shape=[jax.ShapeDtypeStruct((b, t, d_ssm), F32),
                   jax.ShapeDtypeStruct((b, SSM_GROUPS, d_state, gw), F32)],
        scratch_shapes=[pltpu.VMEM((blk + SUBLANES, d_conv), F32),
                        pltpu.VMEM((SSM_GROUPS, d_state, gw), F32)],
        compiler_params=_cparams(("parallel", "arbitrary")),
        name="ssd",
    )(xbc, dt, cprev8, h0t, p['conv_w'], p['conv_b'], p['dt_bias'], p['a_log'], p['dskip_e'], tril,
      p['expand'])


def _attn_kernel(q_ref, kp_ref, kc_ref, vp_ref, vc_ref, bias_ref, o_ref, kwin, vwin,
                 *, past, qb, sub, kw, heads, head_dim, mask_positions):
    c = pl.program_id(1)
    kwin[0:past, :] = kp_ref[0]
    kwin[past:past + qb, :] = kc_ref[0]
    vwin[0:past, :] = vp_ref[0]
    vwin[past:past + qb, :] = vc_ref[0]
    scale = head_dim ** -0.5
    lane = lax.broadcasted_iota(jnp.int32, (sub, LANES), 1)
    first_head = lane < head_dim
    col = lax.broadcasted_iota(jnp.int32, (sub, kw), 1)

    def sub_block(i, carry):
        r0 = pl.multiple_of(i * sub, sub)
        valid = (col + (c * qb - past + i * sub)) >= 0
        for jp in range(heads * head_dim // LANES):
            ls = slice(jp * LANES, (jp + 1) * LANES)
            q2 = q_ref[0, pl.ds(r0, sub), ls]
            k2 = kwin[pl.ds(r0, kw), ls]
            v2 = vwin[pl.ds(r0, kw), ls]
            outs = []
            for half in range(2):
                keep = first_head if half == 0 else jnp.logical_not(first_head)
                qh = jnp.where(keep, q2, jnp.zeros_like(q2))
                s = lax.dot_general(qh, k2, (((1,), (1,)), ((), ())), preferred_element_type=F32)
                s = s * scale + bias_ref[2 * jp + half]
                if mask_positions:
                    s = jnp.where(valid, s, NEG_BIG)
                m = jnp.max(s, axis=-1, keepdims=True)
                e = jnp.exp(s - m)
                l = jnp.sum(e, axis=-1, keepdims=True)
                outs.append(jnp.dot(e.astype(BF16), v2, preferred_element_type=F32) / l)
            o_ref[0, pl.ds(r0, sub), ls] = jnp.where(first_head, outs[0], outs[1])
        return carry

    lax.fori_loop(0, qb // sub, sub_block, 0)


def _attention(q, k, v, k_past, v_past, bias, *, qb, sub, mask_positions, heads):
    b, t, d = q.shape
    kw = bias.shape[2]
    if k_past is None:
        past = qb
        prev_spec = pl.BlockSpec((1, past, d), lambda i, c: (i, jnp.maximum(c - 1, 0), 0))
        k_past, v_past = k, v
    else:
        assert t == qb
        past = k_past.shape[1]
        prev_spec = pl.BlockSpec((1, past, d), lambda i, c: (i, 0, 0))
    cur_spec = pl.BlockSpec((1, qb, d), lambda i, c: (i, c, 0))
    kern = functools.partial(_attn_kernel, past=past, qb=qb, sub=sub, kw=kw, heads=heads,
                             head_dim=d // heads, mask_positions=mask_positions)
    return pl.pallas_call(
        kern,
        grid=(b, t // qb),
        in_specs=[cur_spec, prev_spec, cur_spec, prev_spec, cur_spec, _full(bias.shape)],
        out_specs=pl.BlockSpec((1, qb, d), lambda i, c: (i, c, 0)),
        out_shape=jax.ShapeDtypeStruct((b, t, d), F32),
        scratch_shapes=[pltpu.VMEM((past + qb, d), BF16), pltpu.VMEM((past + qb, d), BF16)],
        compiler_params=_cparams(("parallel", "arbitrary")),
        name="attention",
    )(q, k_past, k, v_past, v, bias)


def _band_bias(table, sub, kw, chunked):
    r = jnp.arange(sub)[:, None]
    s = jnp.arange(kw)[None, :]
    if chunked:
        a = r // CHUNK
        s_loc = s - a * CHUNK
        ok = (s_loc >= 0) & (s_loc < BAND_PAST + CHUNK)
        rel = BAND_PAST + (r % CHUNK) - s_loc
    else:
        ok = jnp.ones((sub, kw), bool)
        rel = (kw - sub) + r - s
    idx = jnp.clip(rel, -REL_CLIP, REL_CLIP) + REL_CLIP
    bias = jnp.transpose(table[idx], (2, 0, 1)).astype(F32)
    return jnp.where(ok[None], bias, NEG_BIG)


def _layer_norm(r, g, b):
    mu = jnp.mean(r, axis=-1, keepdims=True)
    d = r - mu
    var = jnp.mean(d * d, axis=-1, keepdims=True)
    return d * lax.rsqrt(var + NORM_EPS) * g + b


def _post_mixer_kernel(y_ref, z_ref, o_ref, x_ref, wo_ref, gs_ref, ga_ref, lg_ref, lb_ref, rwt_ref, rb_ref,
                       h_ref, hb_ref, comb_ref, cscr, *, alpha, n_experts, d_ssm):
    u = y_ref[...] * _silu(z_ref[...])
    ys = u * lax.rsqrt(jnp.mean(u * u, axis=-1, keepdims=True) + NORM_EPS) * gs_ref[...]
    o = o_ref[...]
    oa = o * lax.rsqrt(jnp.mean(o * o, axis=-1, keepdims=True) + NORM_EPS) * ga_ref[...]
    mix = (jnp.dot(ys.astype(BF16), wo_ref[0:d_ssm, :], preferred_element_type=F32)
           + jnp.dot(oa.astype(BF16), wo_ref[d_ssm:, :], preferred_element_type=F32))
    h = _layer_norm(alpha * x_ref[...] + mix, lg_ref[...], lb_ref[...])
    h_ref[...] = h
    hb_ref[...] = h.astype(BF16)

    hp = _split3(h)
    lt = sum(lax.dot_general(rwt_ref[i], hp[j], (((1,), (1,)), ((), ())), preferred_element_type=F32)
             for i in range(3) for j in range(3 - i))
    rows = [lt[e:e + 1, :] for e in range(n_experts)]
    mx = functools.reduce(jnp.maximum, rows)
    ex = [jnp.exp(r - mx) for r in rows]
    zsum = functools.reduce(lambda p, q: p + q, ex)
    score = [e / zsum for e in ex]
    sel = [score[e] + rb_ref[e] for e in range(n_experts)]
    epg = n_experts // N_EXPERT_GROUPS
    assert epg == 4

    def top2_sum(v):
        a, b = jnp.maximum(v[0], v[1]), jnp.minimum(v[0], v[1])
        c, d = jnp.maximum(v[2], v[3]), jnp.minimum(v[2], v[3])
        return jnp.maximum(a, c) + jnp.maximum(jnp.minimum(a, c), jnp.maximum(b, d))

    gscore = [top2_sum(sel[g * epg:(g + 1) * epg]) for g in range(N_EXPERT_GROUPS)]
    best, gidx = gscore[0], jnp.zeros_like(gscore[0], dtype=jnp.int32)
    for g in range(1, N_EXPERT_GROUPS):
        upd = gscore[g] > best
        best = jnp.where(upd, gscore[g], best)
        gidx = jnp.where(upd, g, gidx)
    in_sel = [functools.reduce(lambda p, q: p + q,
                               [jnp.where(gidx == g, sel[g * epg + j], 0.0) for g in range(N_EXPERT_GROUPS)])
              for j in range(epg)]
    in_score = [functools.reduce(lambda p, q: p + q,
                                 [jnp.where(gidx == g, score[g * epg + j], 0.0) for g in range(N_EXPERT_GROUPS)])
                for j in range(epg)]

    def argmax_first(vals, exclude=None):
        bv, bi = None, None
        for j, v in enumerate(vals):
            if exclude is not None:
                v = jnp.where(exclude == j, -jnp.inf, v)
            if bv is None:
                bv, bi = v, jnp.zeros_like(gidx)
            else:
                upd = v > bv
                bv = jnp.where(upd, v, bv)
                bi = jnp.where(upd, j, bi)
        return bi

    j1 = argmax_first(in_sel)
    j2 = argmax_first(in_sel, exclude=j1)
    s1 = functools.reduce(lambda p, q: p + q, [jnp.where(j1 == j, in_score[j], 0.0) for j in range(epg)])
    s2 = functools.reduce(lambda p, q: p + q, [jnp.where(j2 == j, in_score[j], 0.0) for j in range(epg)])
    g1, g2 = s1 / (s1 + s2), s2 / (s1 + s2)
    e1, e2 = gidx * epg + j1, gidx * epg + j2
    cscr[...] = jnp.zeros_like(cscr)
    for e in range(n_experts):
        cscr[e:e + 1, :] = jnp.where(e1 == e, g1, 0.0) + jnp.where(e2 == e, g2, 0.0)
    comb_ref[...] = cscr[...].T


def _post_mixer(y, z, o, x, p, tm, alpha):
    n, d = x.shape
    d_ssm = y.shape[1]
    row = lambda width: pl.BlockSpec((tm, width), lambda i: (i, 0))
    n_experts = p['router_bias'].shape[0]
    kern = functools.partial(_post_mixer_kernel, alpha=alpha, n_experts=n_experts, d_ssm=d_ssm)
    return pl.pallas_call(
        kern,
        grid=(n // tm,),
        in_specs=[row(d_ssm), row(d_ssm), row(o.shape[1]), row(d), _full(p['w_out'].shape),
                  _full(p['ssd_norm_g'].shape), _full(p['attn_norm_g'].shape), _full(p['ln1_g'].shape),
                  _full(p['ln1_b'].shape), _full(p['router_wt'].shape),
                  pl.BlockSpec(memory_space=pltpu.SMEM)],
        out_specs=[row(d), row(d), row(LANES)],
        out_shape=[jax.ShapeDtypeStruct((n, d), F32), jax.ShapeDtypeStruct((n, d), BF16),
                   jax.ShapeDtypeStruct((n, LANES), F32)],
        scratch_shapes=[pltpu.VMEM((LANES, tm), F32)],
        compiler_params=_cparams(("parallel",)),
        name="post_mixer",
    )(y, z, o, x, p['w_out'], p['ssd_norm_g'], p['attn_norm_g'], p['ln1_g'], p['ln1_b'], p['router_wt'],
      p['router_bias'])


def _moe_kernel(hb_ref, h_ref, comb_ref, wg_ref, wu_ref, wd_ref, lg_ref, lb_ref, out_ref, acc, *, alpha):
    e = pl.program_id(1)

    @pl.when(e == 0)
    def _():
        acc[...] = jnp.zeros_like(acc)

    hb = hb_ref[...]
    gate = jnp.dot(hb, wg_ref[0], preferred_element_type=F32)
    up = jnp.dot(hb, wu_ref[0], preferred_element_type=F32)
    he = (_silu(gate) * up).astype(BF16)
    comb = comb_ref[...]
    lane = lax.broadcasted_iota(jnp.int32, comb.shape, 1)
    w = jnp.sum(jnp.where(lane == e, comb, 0.0), axis=-1, keepdims=True)
    acc[...] += w * jnp.dot(he, wd_ref[0], preferred_element_type=F32)

    @pl.when(e == pl.num_programs(1) - 1)
    def _():
        out_ref[...] = _layer_norm(alpha * h_ref[...] + acc[...], lg_ref[...], lb_ref[...])


def _moe(hb, h, comb, p, tm, alpha):
    n, d = h.shape
    n_experts, _, d_exp = p['w_gate'].shape
    row = lambda width: pl.BlockSpec((tm, width), lambda i, e: (i, 0))
    return pl.pallas_call(
        functools.partial(_moe_kernel, alpha=alpha),
        grid=(n // tm, n_experts),
        in_specs=[row(d), row(d), row(LANES),
                  pl.BlockSpec((1, d, d_exp), lambda i, e: (e, 0, 0)),
                  pl.BlockSpec((1, d, d_exp), lambda i, e: (e, 0, 0)),
                  pl.BlockSpec((1, d_exp, d), lambda i, e: (e, 0, 0)),
                  _full(p['ln2_g'].shape), _full(p['ln2_b'].shape)],
        out_specs=row(d),
        out_shape=jax.ShapeDtypeStruct((n, d), F32),
        scratch_shapes=[pltpu.VMEM((tm, d), F32)],
        compiler_params=_cparams(("parallel", "arbitrary")),
        name="moe",
    )(hb, h, comb, p['w_gate'], p['w_up'], p['w_down'], p['ln2_g'], p['ln2_b'])


def _pick(n, candidates):
    for c in candidates:
        if n % c == 0:
            return c
    raise ValueError(f"no tile size for {n}")


def _layer_params(i, w_in, conv_w, conv_b, dt_bias, a_log, d_skip, ssd_norm_g, attn_norm_g, w_out,
                  ln1_g, ln1_b, router_w, router_bias, w_gate, w_up, w_down, ln2_g, ln2_b, dims):
    d_ssm, d_conv, heads, d_att = dims
    o = 0
    wi = w_in[i]
    wz = wi[:, o:o + d_ssm]; o += d_ssm
    wxbc = wi[:, o:o + d_conv]; o += d_conv
    wdt = wi[:, o:o + heads]; o += heads
    wq = wi[:, o:o + d_att]; o += d_att
    wk = wi[:, o:o + d_att]; o += d_att
    wv = wi[:, o:o + d_att]
    pad_lanes = lambda v: jnp.pad(v, ((0, 0), (0, LANES - v.shape[1])))
    head_dim = d_ssm // heads
    expand = (jnp.arange(LANES)[:, None] == (jnp.arange(d_ssm)[None, :] // head_dim)).astype(BF16)
    return {
        'wz': wz.astype(BF16), 'wxbc': wxbc.astype(BF16), 'wdt': pad_lanes(wdt).astype(BF16),
        'wq': wq.astype(BF16), 'wk': wk.astype(BF16), 'wv': wv.astype(BF16),
        'conv_w': conv_w[i], 'conv_b': conv_b[i][None, :],
        'dt_bias': pad_lanes(dt_bias[i][None, :]), 'a_log': pad_lanes(a_log[i][None, :]),
        'dskip_e': jnp.repeat(d_skip[i], head_dim)[None, :], 'expand': expand,
        'ssd_norm_g': ssd_norm_g[i][None, :], 'attn_norm_g': attn_norm_g[i][None, :],
        'w_out': w_out[i].astype(BF16), 'ln1_g': ln1_g[i][None, :], 'ln1_b': ln1_b[i][None, :],
        'router_wt': jnp.stack(_split3(jnp.pad(router_w.T, ((0, LANES - router_w.shape[1]), (0, 0))))),
        'router_bias': router_bias,
        'w_gate': w_gate[i].astype(BF16), 'w_up': w_up[i].astype(BF16), 'w_down': w_down[i].astype(BF16),
        'ln2_g': ln2_g[i][None, :], 'ln2_b': ln2_b[i][None, :],
    }


def _trunk(x, layers, biases, conv_prev, h0, k_prev, v_prev, alpha, heads):
    b, t, d = x.shape
    n = b * t
    prompt = k_prev is None
    ks, vs, hs, cs = [], [], [], []
    x2 = x.reshape(n, d)
    tm = _pick(n, (256, 128, 64, 32, 16, 8))
    tm_moe = _pick(n, (512, 256, 128, 64, 32, 16, 8))
    for i, p in enumerate(layers):
        z, xbc, dt, q, k, v = _in_proj(x2, p, tm)
        d_conv = xbc.shape[1]
        d_ssm = z.shape[1]
        d_state = (d_conv - d_ssm) // (2 * SSM_GROUPS)
        xbc3 = xbc.reshape(b, t, d_conv)
        cprev8 = jnp.pad(conv_prev[i], ((0, 0), (SUBLANES - (CONV_W - 1), 0), (0, 0)))
        h0t = jnp.swapaxes(h0[i].reshape(b, SSM_GROUPS, d_ssm // SSM_GROUPS, d_state), 2, 3)
        blk = _pick(t, (256, 128, 64, 32, 16, 8))
        y, ht = _ssd(xbc3, dt.reshape(b, t, LANES), cprev8, h0t, p, blk)
        q3, k3, v3 = (a.reshape(b, t, -1) for a in (q, k, v))
        if prompt:
            o = _attention(q3, k3, v3, None, None, biases[i], qb=BAND_PAST, sub=2 * CHUNK,
                           mask_positions=True, heads=heads)
            keep = min(BAND_PAST, t)
            k_state, v_state = k3[:, t - keep:], v3[:, t - keep:]
        else:
            o = _attention(q3, k3, v3, k_prev[i], v_prev[i], biases[i], qb=t, sub=t,
                           mask_positions=False, heads=heads)
            k_state, v_state = k3, v3
        h, hb, comb = _post_mixer(y.reshape(n, d_ssm), z, o.reshape(n, -1), x2, p, tm, alpha)
        x2 = _moe(hb, h, comb, p, tm_moe, alpha)
        head_dim = k3.shape[2] // heads
        ks.append(k_state.astype(F32).reshape(b, -1, heads, head_dim))
        vs.append(v_state.astype(F32).reshape(b, -1, heads, head_dim))
        hs.append(jnp.swapaxes(ht, 2, 3).reshape(h0[i].shape))
        full = jnp.concatenate([conv_prev[i], xbc3], axis=1) if t < CONV_W - 1 else xbc3
        cs.append(full[:, full.shape[1] - (CONV_W - 1):])
    return x2.reshape(b, t, d), jnp.stack(ks), jnp.stack(vs), jnp.stack(hs), jnp.stack(cs)


def kernel(x_prompt, x_sample, cache_k, cache_v, state_ssm, state_conv, w_in, conv_w, conv_b, dt_bias, a_log,
           d_skip, ssd_norm_g, attn_norm_g, rel_bias, w_out, ln1_g, ln1_b, router_w, router_bias, w_gate, w_up,
           w_down, ln2_g, ln2_b):
    depth = w_in.shape[0]
    heads = cache_k.shape[3]
    d_att = heads * cache_k.shape[4]
    d_conv = state_conv.shape[3]
    ssm_heads = state_ssm.shape[2]
    d_ssm = ssm_heads * state_ssm.shape[3]
    alpha = (2 * depth) ** 0.25
    dims = (d_ssm, d_conv, ssm_heads, d_att)
    layers = [_layer_params(i, w_in, conv_w, conv_b, dt_bias, a_log, d_skip, ssd_norm_g, attn_norm_g, w_out,
                            ln1_g, ln1_b, router_w, router_bias, w_gate, w_up, w_down, ln2_g, ln2_b, dims)
              for i in range(depth)]
    bp, tp, _ = x_prompt.shape
    bs, ts, _ = x_sample.shape

    conv0 = jnp.zeros((depth, bp, CONV_W - 1, d_conv), F32)
    h00 = jnp.zeros((depth, bp) + state_ssm.shape[2:], F32)
    bias_p = [_band_bias(rel_bias[i], 2 * CHUNK, BAND_PAST + 2 * CHUNK, True) for i in range(depth)]
    y_p, k_p, v_p, h_p, c_p = _trunk(x_prompt, layers, bias_p, conv0, h00, None, None, alpha, heads)

    past = cache_k.shape[2]
    bias_s = [_band_bias(rel_bias[i], ts, past + ts, False) for i in range(depth)]
    ck = cache_k.reshape(depth, bs, past, d_att).astype(BF16)
    cv = cache_v.reshape(depth, bs, past, d_att).astype(BF16)
    y_s, k_s, v_s, h_s, c_s = _trunk(x_sample, layers, bias_s, state_conv, state_ssm, ck, cv, alpha, heads)
    return (y_p, y_s, k_p, v_p, h_p, c_p, k_s, v_s, h_s, c_s)
```

```python
import functools

import jax
import jax.numpy as jnp
from jax import lax
from jax.experimental import pallas as pl
from jax.experimental.pallas import tpu as pltpu

F32 = jnp.float32
BF16 = jnp.bfloat16

CHUNK = 64
BAND_PREV_CHUNKS = 8
BAND_PAST = BAND_PREV_CHUNKS * CHUNK
REL_CLIP = 256
SSM_GROUPS = 2
CONV_W = 4
N_EXPERT_GROUPS = 4
EXPERTS_PER_GROUP = 4
PAIRS = [(lo, hi) for lo in range(EXPERTS_PER_GROUP) for hi in range(lo + 1, EXPERTS_PER_GROUP)]
PAIRS_PER_GROUP = len(PAIRS)
ROUTE_BUCKET, ROUTE_GATE_LO, ROUTE_GATE_HI = 0, 1, 2
MOE_TILE = 256
NORM_EPS = 1e-5
NEG_BIG = -1e30

LANES = 128
SUBLANES = 8
VMEM_LIMIT = 56 * 1024 * 1024

HIGHEST = lax.Precision.HIGHEST


def _cparams(sem):
    return pltpu.CompilerParams(dimension_semantics=sem, vmem_limit_bytes=VMEM_LIMIT)


def _silu(v):
    return v * jax.nn.sigmoid(v)


def _split3(v):
    p1 = v.astype(BF16)
    r1 = v - p1.astype(F32)
    p2 = r1.astype(BF16)
    p3 = (r1 - p2.astype(F32)).astype(BF16)
    return p1, p2, p3


def _full(shape):
    return pl.BlockSpec(shape, lambda *_: (0,) * len(shape))


def _in_proj_kernel(x_ref, wz_ref, wxbc_ref, wdt_ref, wq_ref, wk_ref, wv_ref,
                    z_ref, xbc_ref, dt_ref, q_ref, k_ref, v_ref):
    xb = x_ref[...].astype(BF16)
    z_ref[...] = jnp.dot(xb, wz_ref[...], preferred_element_type=F32)
    xbc_ref[...] = jnp.dot(xb, wxbc_ref[...], preferred_element_type=F32)
    dt_ref[...] = jnp.dot(xb, wdt_ref[...], preferred_element_type=F32)
    q_ref[...] = jnp.dot(xb, wq_ref[...], preferred_element_type=F32).astype(BF16)
    k_ref[...] = jnp.dot(xb, wk_ref[...], preferred_element_type=F32).astype(BF16)
    v_ref[...] = jnp.dot(xb, wv_ref[...], preferred_element_type=F32).astype(BF16)


def _in_proj(x2d, w, tm):
    n, d = x2d.shape
    d_ssm, d_conv, d_att = w['wz'].shape[1], w['wxbc'].shape[1], w['wq'].shape[1]
    row = lambda width: pl.BlockSpec((tm, width), lambda i: (i, 0))
    return pl.pallas_call(
        _in_proj_kernel,
        grid=(n // tm,),
        in_specs=[row(d), _full(w['wz'].shape), _full(w['wxbc'].shape), _full(w['wdt'].shape),
                  _full(w['wq'].shape), _full(w['wk'].shape), _full(w['wv'].shape)],
        out_specs=[row(d_ssm), row(d_conv), row(LANES), row(d_att), row(d_att), row(d_att)],
        out_shape=[jax.ShapeDtypeStruct((n, d_ssm), F32), jax.ShapeDtypeStruct((n, d_conv), F32),
                   jax.ShapeDtypeStruct((n, LANES), F32), jax.ShapeDtypeStruct((n, d_att), BF16),
                   jax.ShapeDtypeStruct((n, d_att), BF16), jax.ShapeDtypeStruct((n, d_att), BF16)],
        compiler_params=_cparams(("parallel",)),
        name="in_proj",
    )(x2d, w['wz'], w['wxbc'], w['wdt'], w['wq'], w['wk'], w['wv'])


def _ssd_kernel(xbc_ref, dt_ref, cprev_ref, h0_ref, convw_ref, convb_ref, dtb_ref, alog_ref,
                dskip_ref, tril_ref, ex_ref, y_ref, hout_ref, cbuf, hst, *, blk, d_ssm, d_state, head_dim):
    c = pl.program_id(1)
    gw = d_ssm // SSM_GROUPS

    @pl.when(c == 0)
    def _():
        hst[...] = h0_ref[0]
        cbuf[0:SUBLANES, :] = cprev_ref[0]

    cbuf[SUBLANES:SUBLANES + blk, :] = xbc_ref[0]
    acc = jnp.broadcast_to(convb_ref[...], (blk, convb_ref.shape[1]))
    for i in range(CONV_W):
        lo = SUBLANES - (CONV_W - 1) + i
        acc = acc + cbuf[lo:lo + blk, :] * convw_ref[i:i + 1, :]
    cbuf[0:SUBLANES, :] = cbuf[blk:blk + SUBLANES, :]
    xc = _silu(acc)
    xs = xc[:, :d_ssm]
    bms = [xc[:, d_ssm + g * d_state: d_ssm + (g + 1) * d_state].astype(BF16) for g in range(SSM_GROUPS)]
    cms = [xc[:, d_ssm + (SSM_GROUPS + g) * d_state: d_ssm + (SSM_GROUPS + g + 1) * d_state].astype(BF16)
           for g in range(SSM_GROUPS)]

    dtr = dt_ref[0] + dtb_ref[...]
    dt = jnp.maximum(dtr, 0.0) + jnp.log1p(jnp.exp(-jnp.abs(dtr)))
    a = -jnp.exp(alog_ref[...])
    acs = sum(jnp.dot(tril_ref[...], piece, preferred_element_type=F32) for piece in _split3(dt * a))
    acs_t = acs.T
    acs_e = sum(jnp.dot(piece, ex_ref[...], preferred_element_type=F32) for piece in _split3(acs))
    dt_e = sum(jnp.dot(piece, ex_ref[...], preferred_element_type=F32) for piece in _split3(dt))
    last_e = acs_e[blk - 1:blk, :]
    dtx = dt_e * xs
    xw = (jnp.exp(last_e - acs_e) * dtx).astype(BF16)
    dtx = dtx.astype(BF16)
    eacs_e = jnp.exp(acs_e)
    cdec_e = jnp.exp(last_e)

    row = lax.broadcasted_iota(jnp.int32, (blk, blk), 0)
    col = lax.broadcasted_iota(jnp.int32, (blk, blk), 1)
    causal = row >= col
    lane = lax.broadcasted_iota(jnp.int32, (blk, LANES), 1)
    first_head = lane < head_dim
    assert 2 * head_dim == LANES

    pairs_per_group = gw // LANES
    for g in range(SSM_GROUPS):
        hg = hst[g]
        cb = lax.dot_general(cms[g], bms[g], (((1,), (1,)), ((), ())), preferred_element_type=F32)
        y_off = jnp.dot(cms[g], hg.astype(BF16), preferred_element_type=F32)
        for jp in range(pairs_per_group):
            p0 = g * gw + jp * LANES
            h0 = p0 // head_dim
            xpair = dtx[:, p0:p0 + LANES]
            outs = []
            for hh in (h0, h0 + 1):
                seg = acs[:, hh:hh + 1] - acs_t[hh:hh + 1, :]
                dec = jnp.exp(jnp.where(causal, seg, NEG_BIG))
                outs.append(jnp.dot((cb * dec).astype(BF16), xpair, preferred_element_type=F32))
            y_diag = jnp.where(first_head, outs[0], outs[1])
            sl = slice(p0, p0 + LANES)
            y_ref[0, :, sl] = (y_diag + y_off[:, jp * LANES:(jp + 1) * LANES] * eacs_e[:, sl]
                               + dskip_ref[:, sl] * xs[:, sl])
        st = lax.dot_general(bms[g], xw[:, g * gw:(g + 1) * gw], (((0,), (0,)), ((), ())),
                             preferred_element_type=F32)
        hst[g] = hg * cdec_e[:, g * gw:(g + 1) * gw] + st

    @pl.when(c == pl.num_programs(1) - 1)
    def _():
        hout_ref[0] = hst[...]


def _ssd(xbc, dt, cprev8, h0t, p, blk):
    b, t, d_conv = xbc.shape
    d_ssm = p['dskip_e'].shape[1]
    d_state = (d_conv - d_ssm) // (2 * SSM_GROUPS)
    gw = d_ssm // SSM_GROUPS
    tril = jnp.tril(jnp.ones((blk, blk), BF16))
    kern = functools.partial(_ssd_kernel, blk=blk, d_ssm=d_ssm, d_state=d_state,
                             head_dim=d_ssm // p['a_log_heads'])
    return pl.pallas_call(
        kern,
        grid=(b, t // blk),
        in_specs=[pl.BlockSpec((1, blk, d_conv), lambda i, c: (i, c, 0)),
                  pl.BlockSpec((1, blk, LANES), lambda i, c: (i, c, 0)),
                  pl.BlockSpec((1, SUBLANES, d_conv), lambda i, c: (i, 0, 0)),
                  pl.BlockSpec((1, SSM_GROUPS, d_state, gw), lambda i, c: (i, 0, 0, 0)),
                  _full(p['conv_w'].shape), _full(p['conv_b'].shape), _full(p['dt_bias'].shape),
                  _full(p['a_log'].shape), _full(p['dskip_e'].shape), _full(tril.shape),
                  _full(p['expand'].shape)],
        out_specs=[pl.BlockSpec((1, blk, d_ssm), lambda i, c: (i, c, 0)),
                   pl.BlockSpec((1, SSM_GROUPS, d_state, gw), lambda i, c: (i, 0, 0, 0))],
        out_shape=[jax.ShapeDtypeStruct((b, t, d_ssm), F32),
                   jax.ShapeDtypeStruct((b, SSM_GROUPS, d_state, gw), F32)],
        scratch_shapes=[pltpu.VMEM((blk + SUBLANES, d_conv), F32),
                        pltpu.VMEM((SSM_GROUPS, d_state, gw), F32)],
        compiler_params=_cparams(("parallel", "arbitrary")),
        name="ssd",
    )(xbc, dt, cprev8, h0t, p['conv_w'], p['conv_b'], p['dt_bias'], p['a_log'], p['dskip_e'], tril,
      p['expand'])


def _attn_kernel(q_ref, kp_ref, kc_ref, vp_ref, vc_ref, bias_ref, o_ref, kwin, vwin,
                 *, past, qb, sub, kw, heads, head_dim, mask_positions):
    c = pl.program_id(1)
    kwin[0:past, :] = kp_ref[0]
    kwin[past:past + qb, :] = kc_ref[0]
    vwin[0:past, :] = vp_ref[0]
    vwin[past:past + qb, :] = vc_ref[0]
    scale = head_dim ** -0.5
    lane = lax.broadcasted_iota(jnp.int32, (sub, LANES), 1)
    first_head = lane < head_dim
    col = lax.broadcasted_iota(jnp.int32, (sub, kw), 1)

    def sub_block(i, carry):
        r0 = pl.multiple_of(i * sub, sub)
        valid = (col + (c * qb - past + i * sub)) >= 0
        for jp in range(heads * head_dim // LANES):
            ls = slice(jp * LANES, (jp + 1) * LANES)
            q2 = q_ref[0, pl.ds(r0, sub), ls]
            k2 = kwin[pl.ds(r0, kw), ls]
            v2 = vwin[pl.ds(r0, kw), ls]
            outs = []
            for half in range(2):
                keep = first_head if half == 0 else jnp.logical_not(first_head)
                qh = jnp.where(keep, q2, jnp.zeros_like(q2))
                s = lax.dot_general(qh, k2, (((1,), (1,)), ((), ())), preferred_element_type=F32)
                s = s * scale + bias_ref[2 * jp + half]
                if mask_positions:
                    s = jnp.where(valid, s, NEG_BIG)
                m = jnp.max(s, axis=-1, keepdims=True)
                e = jnp.exp(s - m)
                l = jnp.sum(e, axis=-1, keepdims=True)
                outs.append(jnp.dot(e.astype(BF16), v2, preferred_element_type=F32) / l)
            o_ref[0, pl.ds(r0, sub), ls] = jnp.where(first_head, outs[0], outs[1])
        return carry

    lax.fori_loop(0, qb // sub, sub_block, 0)


def _attention(q, k, v, k_past, v_past, bias, *, qb, sub, mask_positions, heads):
    b, t, d = q.shape
    kw = bias.shape[2]
    if k_past is None:
        past = qb
        prev_spec = pl.BlockSpec((1, past, d), lambda i, c: (i, jnp.maximum(c - 1, 0), 0))
        k_past, v_past = k, v
    else:
        assert t == qb
        past = k_past.shape[1]
        prev_spec = pl.BlockSpec((1, past, d), lambda i, c: (i, 0, 0))
    cur_spec = pl.BlockSpec((1, qb, d), lambda i, c: (i, c, 0))
    kern = functools.partial(_attn_kernel, past=past, qb=qb, sub=sub, kw=kw, heads=heads,
                             head_dim=d // heads, mask_positions=mask_positions)
    return pl.pallas_call(
        kern,
        grid=(b, t // qb),
        in_specs=[cur_spec, prev_spec, cur_spec, prev_spec, cur_spec, _full(bias.shape)],
        out_specs=pl.BlockSpec((1, qb, d), lambda i, c: (i, c, 0)),
        out_shape=jax.ShapeDtypeStruct((b, t, d), F32),
        scratch_shapes=[pltpu.VMEM((past + qb, d), BF16), pltpu.VMEM((past + qb, d), BF16)],
        compiler_params=_cparams(("parallel", "arbitrary")),
        name="attention",
    )(q, k_past, k, v_past, v, bias)


def _band_bias(table, sub, kw, chunked):
    r = jnp.arange(sub)[:, None]
    s = jnp.arange(kw)[None, :]
    if chunked:
        a = r // CHUNK
        s_loc = s - a * CHUNK
        ok = (s_loc >= 0) & (s_loc < BAND_PAST + CHUNK)
        rel = BAND_PAST + (r % CHUNK) - s_loc
    else:
        ok = jnp.ones((sub, kw), bool)
        rel = (kw - sub) + r - s
    idx = jnp.clip(rel, -REL_CLIP, REL_CLIP) + REL_CLIP
    bias = jnp.transpose(table[idx], (2, 0, 1)).astype(F32)
    return jnp.where(ok[None], bias, NEG_BIG)


def _layer_norm(r, g, b):
    mu = jnp.mean(r, axis=-1, keepdims=True)
    d = r - mu
    var = jnp.mean(d * d, axis=-1, keepdims=True)
    return d * lax.rsqrt(var + NORM_EPS) * g + b


def _post_mixer_kernel(y_ref, z_ref, o_ref, x_ref, wo_ref, gs_ref, ga_ref, lg_ref, lb_ref, rwt_ref, rb_ref,
                       hx_ref, cscr, *, alpha, n_experts, d_ssm):
    u = y_ref[...] * _silu(z_ref[...])
    ys = u * lax.rsqrt(jnp.mean(u * u, axis=-1, keepdims=True) + NORM_EPS) * gs_ref[...]
    o = o_ref[...]
    oa = o * lax.rsqrt(jnp.mean(o * o, axis=-1, keepdims=True) + NORM_EPS) * ga_ref[...]
    mix = (jnp.dot(ys.astype(BF16), wo_ref[0:d_ssm, :], preferred_element_type=F32)
           + jnp.dot(oa.astype(BF16), wo_ref[d_ssm:, :], preferred_element_type=F32))
    h = _layer_norm(alpha * x_ref[...] + mix, lg_ref[...], lb_ref[...])

    hp = _split3(h)
    lt = sum(lax.dot_general(rwt_ref[i], hp[j], (((1,), (1,)), ((), ())), preferred_element_type=F32)
             for i in range(3) for j in range(3 - i))
    rows = [lt[e:e + 1, :] for e in range(n_experts)]
    mx = functools.reduce(jnp.maximum, rows)
    ex = [jnp.exp(r - mx) for r in rows]
    zsum = functools.reduce(lambda p, q: p + q, ex)
    score = [e / zsum for e in ex]
    sel = [score[e] + rb_ref[e] for e in range(n_experts)]
    epg = EXPERTS_PER_GROUP
    assert n_experts == N_EXPERT_GROUPS * epg and epg == 4

    def top2_sum(v):
        a, b = jnp.maximum(v[0], v[1]), jnp.minimum(v[0], v[1])
        c, d = jnp.maximum(v[2], v[3]), jnp.minimum(v[2], v[3])
        return jnp.maximum(a, c) + jnp.maximum(jnp.minimum(a, c), jnp.maximum(b, d))

    gscore = [top2_sum(sel[g * epg:(g + 1) * epg]) for g in range(N_EXPERT_GROUPS)]
    best, gidx = gscore[0], jnp.zeros_like(gscore[0], dtype=jnp.int32)
    for g in range(1, N_EXPERT_GROUPS):
        upd = gscore[g] > best
        best = jnp.where(upd, gscore[g], best)
        gidx = jnp.where(upd, g, gidx)
    in_sel = [functools.reduce(lambda p, q: p + q,
                               [jnp.where(gidx == g, sel[g * epg + j], 0.0) for g in range(N_EXPERT_GROUPS)])
              for j in range(epg)]
    in_score = [functools.reduce(lambda p, q: p + q,
                                 [jnp.where(gidx == g, score[g * epg + j], 0.0) for g in range(N_EXPERT_GROUPS)])
                for j in range(epg)]

    def argmax_first(vals, exclude=None):
        bv, bi = None, None
        for j, v in enumerate(vals):
            if exclude is not None:
                v = jnp.where(exclude == j, -jnp.inf, v)
            if bv is None:
                bv, bi = v, jnp.zeros_like(gidx)
            else:
                upd = v > bv
                bv = jnp.where(upd, v, bv)
                bi = jnp.where(upd, j, bi)
        return bi

    j1 = argmax_first(in_sel)
    j2 = argmax_first(in_sel, exclude=j1)
    s1 = functools.reduce(lambda p, q: p + q, [jnp.where(j1 == j, in_score[j], 0.0) for j in range(epg)])
    s2 = functools.reduce(lambda p, q: p + q, [jnp.where(j2 == j, in_score[j], 0.0) for j in range(epg)])
    g1, g2 = s1 / (s1 + s2), s2 / (s1 + s2)
    first_lo = j1 < j2
    lo, hi = jnp.where(first_lo, j1, j2), jnp.where(first_lo, j2, j1)
    pair = jnp.where(lo == 0, hi - 1, jnp.where(lo == 1, hi + 1, PAIRS_PER_GROUP - 1))
    bucket = gidx * PAIRS_PER_GROUP + pair
    cscr[...] = jnp.zeros_like(cscr)
    cscr[ROUTE_BUCKET:ROUTE_BUCKET + 1, :] = bucket.astype(F32)
    cscr[ROUTE_GATE_LO:ROUTE_GATE_LO + 1, :] = jnp.where(first_lo, g1, g2)
    cscr[ROUTE_GATE_HI:ROUTE_GATE_HI + 1, :] = jnp.where(first_lo, g2, g1)
    d = h.shape[1]
    hx_ref[:, :d] = h
    hx_ref[:, d:] = cscr[...].T


def _post_mixer(y, z, o, x, p, tm, alpha):
    n, d = x.shape
    d_ssm = y.shape[1]
    row = lambda width: pl.BlockSpec((tm, width), lambda i: (i, 0))
    n_experts = p['router_bias'].shape[0]
    kern = functools.partial(_post_mixer_kernel, alpha=alpha, n_experts=n_experts, d_ssm=d_ssm)
    return pl.pallas_call(
        kern,
        grid=(n // tm,),
        in_specs=[row(d_ssm), row(d_ssm), row(o.shape[1]), row(d), _full(p['w_out'].shape),
                  _full(p['ssd_norm_g'].shape), _full(p['attn_norm_g'].shape), _full(p['ln1_g'].shape),
                  _full(p['ln1_b'].shape), _full(p['router_wt'].shape),
                  pl.BlockSpec(memory_space=pltpu.SMEM)],
        out_specs=row(d + LANES),
        out_shape=jax.ShapeDtypeStruct((n, d + LANES), F32),
        scratch_shapes=[pltpu.VMEM((LANES, tm), F32)],
        compiler_params=_cparams(("parallel",)),
        name="post_mixer",
    )(y, z, o, x, p['w_out'], p['ssd_norm_g'], p['attn_norm_g'], p['ln1_g'], p['ln1_b'], p['router_wt'],
      p['router_bias'])


def _start_row_gather(idx_ref, tile, src_hbm, buf, sem, slot, rows):
    def body(r, carry):
        row = idx_ref[tile, r]
        pltpu.make_async_copy(src_hbm.at[pl.ds(row, 1), :], buf.at[slot, pl.ds(r, 1), :], sem.at[slot]).start()
        return carry
    lax.fori_loop(0, rows, body, 0, unroll=8)


def _wait_row_gather(src_hbm, buf, sem, slot, rows):
    pltpu.make_async_copy(src_hbm.at[pl.ds(0, rows), :], buf.at[slot], sem.at[slot]).wait()


def _prefetched_rows(idx_ref, src_hbm, buf, sem, rows):
    i = pl.program_id(0)
    slot = lax.rem(i, 2)

    @pl.when(i == 0)
    def _():
        _start_row_gather(idx_ref, 0, src_hbm, buf, sem, 0, rows)

    @pl.when(i + 1 < pl.num_programs(0))
    def _():
        _start_row_gather(idx_ref, i + 1, src_hbm, buf, sem, 1 - slot, rows)

    _wait_row_gather(src_hbm, buf, sem, slot, rows)
    return slot


def _moe_kernel(src_ref, elo_ref, ehi_ref, nused_ref, hx_hbm, wg_lo, wu_lo, wd_lo, wg_hi, wu_hi, wd_hi,
                y_ref, xbuf, sem, *, tm, d):
    del elo_ref, ehi_ref
    slot = _prefetched_rows(src_ref, hx_hbm, xbuf, sem, tm)

    @pl.when(pl.program_id(0) < nused_ref[0])
    def _():
        x = xbuf[slot]
        xb = x[:, :d].astype(BF16)

        def ffn(wg, wu, wd):
            gate = jnp.dot(xb, wg[0], preferred_element_type=F32)
            up = jnp.dot(xb, wu[0], preferred_element_type=F32)
            return jnp.dot((_silu(gate) * up).astype(BF16), wd[0], preferred_element_type=F32)

        y_ref[...] = (x[:, d + ROUTE_GATE_LO:d + ROUTE_GATE_LO + 1] * ffn(wg_lo, wu_lo, wd_lo)
                      + x[:, d + ROUTE_GATE_HI:d + ROUTE_GATE_HI + 1] * ffn(wg_hi, wu_hi, wd_hi))

    @pl.when(pl.program_id(0) >= nused_ref[0])
    def _():
        y_ref[...] = jnp.zeros_like(y_ref)


def _moe(hx, src2d, elo, ehi, nused, p, d):
    n_tiles, tm = src2d.shape
    _, _, d_exp = p['w_gate'].shape
    wspec = lambda sel: pl.BlockSpec((1, d, d_exp), lambda i, s, lo, hi, nu: ((lo if sel == 0 else hi)[i], 0, 0))
    wdspec = lambda sel: pl.BlockSpec((1, d_exp, d), lambda i, s, lo, hi, nu: ((lo if sel == 0 else hi)[i], 0, 0))
    return pl.pallas_call(
        functools.partial(_moe_kernel, tm=tm, d=d),
        grid_spec=pltpu.PrefetchScalarGridSpec(
            num_scalar_prefetch=4,
            grid=(n_tiles,),
            in_specs=[pl.BlockSpec(memory_space=pl.ANY), wspec(0), wspec(0), wdspec(0),
                      wspec(1), wspec(1), wdspec(1)],
            out_specs=pl.BlockSpec((tm, d), lambda i, *_: (i, 0)),
            scratch_shapes=[pltpu.VMEM((2, tm, hx.shape[1]), F32), pltpu.SemaphoreType.DMA((2,))]),
        out_shape=jax.ShapeDtypeStruct((n_tiles * tm, d), F32),
        compiler_params=_cparams(("arbitrary",)),
        name="moe",
    )(src2d, elo, ehi, nused, hx, p['w_gate'], p['w_up'], p['w_down'], p['w_gate'], p['w_up'], p['w_down'])


def _finalize_kernel(pos_ref, y_hbm, hx_ref, lg_ref, lb_ref, out_ref, ybuf, sem, *, tm, d, alpha):
    slot = _prefetched_rows(pos_ref, y_hbm, ybuf, sem, tm)
    out_ref[...] = _layer_norm(alpha * hx_ref[:, :d] + ybuf[slot], lg_ref[...], lb_ref[...])


def _finalize(y_sorted, pos2d, hx, p, d, alpha):
    n_tiles, tm = pos2d.shape
    return pl.pallas_call(
        functools.partial(_finalize_kernel, tm=tm, d=d, alpha=alpha),
        grid_spec=pltpu.PrefetchScalarGridSpec(
            num_scalar_prefetch=1,
            grid=(n_tiles,),
            in_specs=[pl.BlockSpec(memory_space=pl.ANY),
                      pl.BlockSpec((tm, hx.shape[1]), lambda i, pos: (i, 0)),
                      pl.BlockSpec(p['ln2_g'].shape, lambda i, pos: (0, 0)),
                      pl.BlockSpec(p['ln2_b'].shape, lambda i, pos: (0, 0))],
            out_specs=pl.BlockSpec((tm, d), lambda i, pos: (i, 0)),
            scratch_shapes=[pltpu.VMEM((2, tm, d), F32), pltpu.SemaphoreType.DMA((2,))]),
        out_shape=jax.ShapeDtypeStruct((n_tiles * tm, d), F32),
        compiler_params=_cparams(("arbitrary",)),
        name="finalize",
    )(pos2d, y_sorted, hx, p['ln2_g'], p['ln2_b'])


def _route_tables(bucket, n_groups, tm):
    n = bucket.shape[0]
    n_buckets = n_groups * PAIRS_PER_GROUP
    n_tiles = n // tm + n_buckets
    onehot = (bucket[:, None] == jnp.arange(n_buckets, dtype=jnp.int32)[None, :]).astype(jnp.int32)
    csum = jnp.cumsum(onehot, axis=0)
    counts = csum[-1]
    rank = jnp.take_along_axis(csum, bucket[:, None], axis=1)[:, 0] - 1
    tiles_per = (counts + tm - 1) // tm
    tile_end = jnp.cumsum(tiles_per)
    pos = (tile_end - tiles_per)[bucket] * tm + rank
    src = jnp.zeros((n_tiles * tm,), jnp.int32).at[pos].set(jnp.arange(n, dtype=jnp.int32), unique_indices=True)
    tile_bucket = jnp.minimum(jnp.searchsorted(tile_end, jnp.arange(n_tiles, dtype=jnp.int32), side='right'),
                              n_buckets - 1).astype(jnp.int32)
    group, pair = tile_bucket // PAIRS_PER_GROUP, tile_bucket % PAIRS_PER_GROUP
    elo = group * EXPERTS_PER_GROUP + jnp.asarray([lo for lo, _ in PAIRS], jnp.int32)[pair]
    ehi = group * EXPERTS_PER_GROUP + jnp.asarray([hi for _, hi in PAIRS], jnp.int32)[pair]
    return pos, src.reshape(n_tiles, tm), elo, ehi, tile_end[-1:].astype(jnp.int32)


def _pick(n, candidates):
    for c in candidates:
        if n % c == 0:
            return c
    raise ValueError(f"no tile size for {n}")


def _layer_params(i, w_in, conv_w, conv_b, dt_bias, a_log, d_skip, ssd_norm_g, attn_norm_g, w_out,
                  ln1_g, ln1_b, router_w, router_bias, w_gate, w_up, w_down, ln2_g, ln2_b, dims):
    d_ssm, d_conv, heads, d_att = dims
    o = 0
    wi = w_in[i]
    wz = wi[:, o:o + d_ssm]; o += d_ssm
    wxbc = wi[:, o:o + d_conv]; o += d_conv
    wdt = wi[:, o:o + heads]; o += heads
    wq = wi[:, o:o + d_att]; o += d_att
    wk = wi[:, o:o + d_att]; o += d_att
    wv = wi[:, o:o + d_att]
    pad_lanes = lambda v: jnp.pad(v, ((0, 0), (0, LANES - v.shape[1])))
    head_dim = d_ssm // heads
    expand = (jnp.arange(LANES)[:, None] == (jnp.arange(d_ssm)[None, :] // head_dim)).astype(BF16)
    return {
        'wz': wz.astype(BF16), 'wxbc': wxbc.astype(BF16), 'wdt': pad_lanes(wdt).astype(BF16),
        'wq': wq.astype(BF16), 'wk': wk.astype(BF16), 'wv': wv.astype(BF16),
        'conv_w': conv_w[i], 'conv_b': conv_b[i][None, :],
        'dt_bias': pad_lanes(dt_bias[i][None, :]), 'a_log': pad_lanes(a_log[i][None, :]),
        'dskip_e': jnp.repeat(d_skip[i], head_dim)[None, :], 'expand': expand, 'a_log_heads': heads,
        'ssd_norm_g': ssd_norm_g[i][None, :], 'attn_norm_g': attn_norm_g[i][None, :],
        'w_out': w_out[i].astype(BF16), 'ln1_g': ln1_g[i][None, :], 'ln1_b': ln1_b[i][None, :],
        'router_wt': jnp.stack(_split3(jnp.pad(router_w.T, ((0, LANES - router_w.shape[1]), (0, 0))))),
        'router_bias': router_bias,
        'w_gate': w_gate[i].astype(BF16), 'w_up': w_up[i].astype(BF16), 'w_down': w_down[i].astype(BF16),
        'ln2_g': ln2_g[i][None, :], 'ln2_b': ln2_b[i][None, :],
    }


def _trunk(x, layers, biases, conv_prev, h0, k_prev, v_prev, alpha, heads):
    b, t, d = x.shape
    n = b * t
    prompt = k_prev is None
    ks, vs, hs, cs = [], [], [], []
    x2 = x.reshape(n, d)
    tm = _pick(n, (256, 128, 64, 32, 16, 8))
    tm_fin = _pick(n, (512, 256))
    assert n % MOE_TILE == 0
    for i, p in enumerate(layers):
        z, xbc, dt, q, k, v = _in_proj(x2, p, tm)
        d_conv = xbc.shape[1]
        d_ssm = z.shape[1]
        d_state = (d_conv - d_ssm) // (2 * SSM_GROUPS)
        xbc3 = xbc.reshape(b, t, d_conv)
        cprev8 = jnp.pad(conv_prev[i], ((0, 0), (SUBLANES - (CONV_W - 1), 0), (0, 0)))
        h0t = jnp.swapaxes(h0[i].reshape(b, SSM_GROUPS, d_ssm // SSM_GROUPS, d_state), 2, 3)
        blk = _pick(t, (256, 128, 64, 32, 16, 8))
        y, ht = _ssd(xbc3, dt.reshape(b, t, LANES), cprev8, h0t, p, blk)
        q3, k3, v3 = (a.reshape(b, t, -1) for a in (q, k, v))
        if prompt:
            o = _attention(q3, k3, v3, None, None, biases[i], qb=BAND_PAST, sub=2 * CHUNK,
                           mask_positions=True, heads=heads)
            keep = min(BAND_PAST, t)
            k_state, v_state = k3[:, t - keep:], v3[:, t - keep:]
        else:
            o = _attention(q3, k3, v3, k_prev[i], v_prev[i], biases[i], qb=t, sub=t,
                           mask_positions=False, heads=heads)
            k_state, v_state = k3, v3
        hx = _post_mixer(y.reshape(n, d_ssm), z, o.reshape(n, -1), x2, p, tm, alpha)
        bucket = hx[:, d + ROUTE_BUCKET].astype(jnp.int32)
        pos, src2d, elo, ehi, nused = _route_tables(bucket, N_EXPERT_GROUPS, MOE_TILE)
        y_sorted = _moe(hx, src2d, elo, ehi, nused, p, d)
        x2 = _finalize(y_sorted, pos.reshape(n // tm_fin, tm_fin), hx, p, d, alpha)
        head_dim = k3.shape[2] // heads
        ks.append(k_state.astype(F32).reshape(b, -1, heads, head_dim))
        vs.append(v_state.astype(F32).reshape(b, -1, heads, head_dim))
        hs.append(jnp.swapaxes(ht, 2, 3).reshape(h0[i].shape))
        full = jnp.concatenate([conv_prev[i], xbc3], axis=1) if t < CONV_W - 1 else xbc3
        cs.append(full[:, full.shape[1] - (CONV_W - 1):])
    return x2.reshape(b, t, d), jnp.stack(ks), jnp.stack(vs), jnp.stack(hs), jnp.stack(cs)


def kernel(x_prompt, x_sample, cache_k, cache_v, state_ssm, state_conv, w_in, conv_w, conv_b, dt_bias, a_log,
           d_skip, ssd_norm_g, attn_norm_g, rel_bias, w_out, ln1_g, ln1_b, router_w, router_bias, w_gate, w_up,
           w_down, ln2_g, ln2_b):
    depth = w_in.shape[0]
    heads = cache_k.shape[3]
    d_att = heads * cache_k.shape[4]
    d_conv = state_conv.shape[3]
    ssm_heads = state_ssm.shape[2]
    d_ssm = ssm_heads * state_ssm.shape[3]
    alpha = (2 * depth) ** 0.25
    dims = (d_ssm, d_conv, ssm_heads, d_att)
    layers = [_layer_params(i, w_in, conv_w, conv_b, dt_bias, a_log, d_skip, ssd_norm_g, attn_norm_g, w_out,
                            ln1_g, ln1_b, router_w, router_bias, w_gate, w_up, w_down, ln2_g, ln2_b, dims)
              for i in range(depth)]
    bp, tp, _ = x_prompt.shape
    bs, ts, _ = x_sample.shape

    conv0 = jnp.zeros((depth, bp, CONV_W - 1, d_conv), F32)
    h00 = jnp.zeros((depth, bp) + state_ssm.shape[2:], F32)
    bias_p = [_band_bias(rel_bias[i], 2 * CHUNK, BAND_PAST + 2 * CHUNK, True) for i in range(depth)]
    y_p, k_p, v_p, h_p, c_p = _trunk(x_prompt, layers, bias_p, conv0, h00, None, None, alpha, heads)

    past = cache_k.shape[2]
    bias_s = [_band_bias(rel_bias[i], ts, past + ts, False) for i in range(depth)]
    ck = cache_k.reshape(depth, bs, past, d_att).astype(BF16)
    cv = cache_v.reshape(depth, bs, past, d_att).astype(BF16)
    y_s, k_s, v_s, h_s, c_s = _trunk(x_sample, layers, bias_s, state_conv, state_ssm, ck, cv, alpha, heads)
    return (y_p, y_s, k_p, v_p, h_p, c_p, k_s, v_s, h_s, c_s)
```

```python
import functools

import jax
import jax.numpy as jnp
from jax import lax
from jax.experimental import pallas as pl
from jax.experimental.pallas import tpu as pltpu

F32 = jnp.float32
BF16 = jnp.bfloat16

CHUNK = 64
BAND_PREV_CHUNKS = 8
BAND_PAST = BAND_PREV_CHUNKS * CHUNK
REL_CLIP = 256
SSM_GROUPS = 2
CONV_W = 4
N_EXPERT_GROUPS = 4
EXPERTS_PER_GROUP = 4
PAIRS = [(lo, hi) for lo in range(EXPERTS_PER_GROUP) for hi in range(lo + 1, EXPERTS_PER_GROUP)]
PAIRS_PER_GROUP = len(PAIRS)
ROUTE_BUCKET, ROUTE_GATE_LO, ROUTE_GATE_HI = 0, 1, 2
MOE_TILE = 256
ATT_SUB = 4 * CHUNK
NORM_EPS = 1e-5
NEG_BIG = -1e30

LANES = 128
SUBLANES = 8
VMEM_LIMIT = 56 * 1024 * 1024
ROUTER_ROWS = SUBLANES * EXPERTS_PER_GROUP


def _cparams(sem):
    return pltpu.CompilerParams(dimension_semantics=sem, vmem_limit_bytes=VMEM_LIMIT)


def _silu(v):
    return v * jax.nn.sigmoid(v)


def _split3(v):
    p1 = v.astype(BF16)
    r1 = v - p1.astype(F32)
    p2 = r1.astype(BF16)
    p3 = (r1 - p2.astype(F32)).astype(BF16)
    return p1, p2, p3


def _full(shape):
    return pl.BlockSpec(shape, lambda *_: (0,) * len(shape))


def _in_proj_kernel(x_ref, wz_ref, wxbc_ref, wdt_ref, wq_ref, wk_ref, wv_ref,
                    z_ref, xbc_ref, dt_ref, q_ref, k_ref, v_ref, *, q_scale):
    xb = x_ref[...].astype(BF16)
    z_ref[...] = jnp.dot(xb, wz_ref[...], preferred_element_type=F32)
    xbc_ref[...] = jnp.dot(xb, wxbc_ref[...], preferred_element_type=F32)
    dt_ref[...] = jnp.dot(xb, wdt_ref[...], preferred_element_type=F32)
    q_ref[...] = (jnp.dot(xb, wq_ref[...], preferred_element_type=F32) * q_scale).astype(BF16)
    k_ref[...] = jnp.dot(xb, wk_ref[...], preferred_element_type=F32).astype(BF16)
    v_ref[...] = jnp.dot(xb, wv_ref[...], preferred_element_type=F32).astype(BF16)


def _in_proj(x2d, w, tm):
    n, d = x2d.shape
    d_ssm, d_conv, d_att = w['wz'].shape[1], w['wxbc'].shape[1], w['wq'].shape[1]
    row = lambda width: pl.BlockSpec((tm, width), lambda i: (i, 0))
    return pl.pallas_call(
        functools.partial(_in_proj_kernel, q_scale=w['q_scale']),
        grid=(n // tm,),
        in_specs=[row(d), _full(w['wz'].shape), _full(w['wxbc'].shape), _full(w['wdt'].shape),
                  _full(w['wq'].shape), _full(w['wk'].shape), _full(w['wv'].shape)],
        out_specs=[row(d_ssm), row(d_conv), row(LANES), row(d_att), row(d_att), row(d_att)],
        out_shape=[jax.ShapeDtypeStruct((n, d_ssm), F32), jax.ShapeDtypeStruct((n, d_conv), F32),
                   jax.ShapeDtypeStruct((n, LANES), F32), jax.ShapeDtypeStruct((n, d_att), BF16),
                   jax.ShapeDtypeStruct((n, d_att), BF16), jax.ShapeDtypeStruct((n, d_att), BF16)],
        compiler_params=_cparams(("parallel",)),
        name="in_proj",
    )(x2d, w['wz'], w['wxbc'], w['wdt'], w['wq'], w['wk'], w['wv'])


def _ssd_kernel(xbc_ref, dt_ref, cprev_ref, h0_ref, convw_ref, convb_ref, dtb_ref, alog_ref,
                dskip_ref, tril_ref, ex_ref, y_ref, hout_ref, cbuf, hst, *, blk, d_ssm, d_state, head_dim):
    c = pl.program_id(1)
    gw = d_ssm // SSM_GROUPS

    @pl.when(c == 0)
    def _():
        hst[...] = h0_ref[0]
        cbuf[0:SUBLANES, :] = cprev_ref[0]

    cbuf[SUBLANES:SUBLANES + blk, :] = xbc_ref[0]
    acc = jnp.broadcast_to(convb_ref[...], (blk, convb_ref.shape[1]))
    for i in range(CONV_W):
        lo = SUBLANES - (CONV_W - 1) + i
        acc = acc + cbuf[lo:lo + blk, :] * convw_ref[i:i + 1, :]
    cbuf[0:SUBLANES, :] = cbuf[blk:blk + SUBLANES, :]
    xc = _silu(acc)
    xs = xc[:, :d_ssm]
    bms = [xc[:, d_ssm + g * d_state: d_ssm + (g + 1) * d_state].astype(BF16) for g in range(SSM_GROUPS)]
    cms = [xc[:, d_ssm + (SSM_GROUPS + g) * d_state: d_ssm + (SSM_GROUPS + g + 1) * d_state].astype(BF16)
           for g in range(SSM_GROUPS)]

    dtr = dt_ref[0] + dtb_ref[...]
    dt = jnp.maximum(dtr, 0.0) + jnp.log1p(jnp.exp(-jnp.abs(dtr)))
    a = -jnp.exp(alog_ref[...])
    acs = sum(jnp.dot(tril_ref[...], piece, preferred_element_type=F32) for piece in _split3(dt * a))
    acs_t = acs.T
    acs_e = sum(jnp.dot(piece, ex_ref[...], preferred_element_type=F32) for piece in _split3(acs))
    dt_e = sum(jnp.dot(piece, ex_ref[...], preferred_element_type=F32) for piece in _split3(dt))
    last_e = acs_e[blk - 1:blk, :]
    dtx = dt_e * xs
    xw = (jnp.exp(last_e - acs_e) * dtx).astype(BF16)
    dtx = dtx.astype(BF16)
    eacs_e = jnp.exp(acs_e)
    cdec_e = jnp.exp(last_e)

    row = lax.broadcasted_iota(jnp.int32, (blk, blk), 0)
    col = lax.broadcasted_iota(jnp.int32, (blk, blk), 1)
    causal = row >= col
    lane = lax.broadcasted_iota(jnp.int32, (blk, LANES), 1)
    first_head = lane < head_dim
    assert 2 * head_dim == LANES

    pairs_per_group = gw // LANES
    for g in range(SSM_GROUPS):
        hg = hst[g]
        cb = lax.dot_general(cms[g], bms[g], (((1,), (1,)), ((), ())), preferred_element_type=F32)
        y_off = jnp.dot(cms[g], hg.astype(BF16), preferred_element_type=F32)
        for jp in range(pairs_per_group):
            p0 = g * gw + jp * LANES
            h0 = p0 // head_dim
            xpair = dtx[:, p0:p0 + LANES]
            outs = []
            for hh in (h0, h0 + 1):
                seg = acs[:, hh:hh + 1] - acs_t[hh:hh + 1, :]
                dec = jnp.exp(jnp.where(causal, seg, NEG_BIG))
                outs.append(jnp.dot((cb * dec).astype(BF16), xpair, preferred_element_type=F32))
            y_diag = jnp.where(first_head, outs[0], outs[1])
            sl = slice(p0, p0 + LANES)
            y_ref[0, :, sl] = (y_diag + y_off[:, jp * LANES:(jp + 1) * LANES] * eacs_e[:, sl]
                               + dskip_ref[:, sl] * xs[:, sl])
        st = lax.dot_general(bms[g], xw[:, g * gw:(g + 1) * gw], (((0,), (0,)), ((), ())),
                             preferred_element_type=F32)
        hst[g] = hg * cdec_e[:, g * gw:(g + 1) * gw] + st

    @pl.when(c == pl.num_programs(1) - 1)
    def _():
        hout_ref[0] = hst[...]


def _ssd(xbc, dt, cprev8, h0t, p, blk):
    b, t, d_conv = xbc.shape
    d_ssm = p['dskip_e'].shape[1]
    d_state = (d_conv - d_ssm) // (2 * SSM_GROUPS)
    gw = d_ssm // SSM_GROUPS
    tril = jnp.tril(jnp.ones((blk, blk), BF16))
    kern = functools.partial(_ssd_kernel, blk=blk, d_ssm=d_ssm, d_state=d_state,
                             head_dim=d_ssm // p['a_log_heads'])
    return pl.pallas_call(
        kern,
        grid=(b, t // blk),
        in_specs=[pl.BlockSpec((1, blk, d_conv), lambda i, c: (i, c, 0)),
                  pl.BlockSpec((1, blk, LANES), lambda i, c: (i, c, 0)),
                  pl.BlockSpec((1, SUBLANES, d_conv), lambda i, c: (i, 0, 0)),
                  pl.BlockSpec((1, SSM_GROUPS, d_state, gw), lambda i, c: (i, 0, 0, 0)),
                  _full(p['conv_w'].shape), _full(p['conv_b'].shape), _full(p['dt_bias'].shape),
                  _full(p['a_log'].shape), _full(p['dskip_e'].shape), _full(tril.shape),
                  _full(p['expand'].shape)],
        out_specs=[pl.BlockSpec((1, blk, d_ssm), lambda i, c: (i, c, 0)),
                   pl.BlockSpec((1, SSM_GROUPS, d_state, gw), lambda i, c: (i, 0, 0, 0))],
        out_shape=[jax.ShapeDtypeStruct((b, t, d_ssm), F32),
                   jax.ShapeDtypeStruct((b, SSM_GROUPS, d_state, gw), F32)],
        scratch_shapes=[pltpu.VMEM((blk + SUBLANES, d_conv), F32),
                        pltpu.VMEM((SSM_GROUPS, d_state, gw), F32)],
        compiler_params=_cparams(("parallel", "arbitrary")),
        name="ssd",
    )(xbc, dt, cprev8, h0t, p['conv_w'], p['conv_b'], p['dt_bias'], p['a_log'], p['dskip_e'], tril,
      p['expand'])


def _attn_kernel(q_ref, kp_ref, kc_ref, vp_ref, vc_ref, bias_ref, o_ref, kwin, vwin,
                 *, past, qb, sub, kw, heads, head_dim, mask_positions):
    c = pl.program_id(1)
    kwin[0:past, :] = kp_ref[0]
    kwin[past:past + qb, :] = kc_ref[0]
    vwin[0:past, :] = vp_ref[0]
    vwin[past:past + qb, :] = vc_ref[0]
    lane = lax.broadcasted_iota(jnp.int32, (sub, LANES), 1)
    first_head = lane < head_dim

    def sub_block(i, masked):
        r0 = pl.multiple_of(i * sub, sub)
        if masked:
            col = lax.broadcasted_iota(jnp.int32, (2 * sub, kw), 1)
            valid = (col + (c * qb - past + i * sub)) >= 0
        n_pairs = heads * head_dim // LANES

        def scores(jp):
            ls = slice(jp * LANES, (jp + 1) * LANES)
            q2 = q_ref[0, pl.ds(r0, sub), ls]
            k2 = kwin[pl.ds(r0, kw), ls]
            zero = jnp.zeros_like(q2)
            qq = jnp.concatenate([jnp.where(first_head, q2, zero), jnp.where(first_head, zero, q2)], axis=0)
            return lax.dot_general(qq, k2, (((1,), (1,)), ((), ())), preferred_element_type=F32)

        s_next = scores(0)
        for jp in range(n_pairs):
            s = s_next
            if jp + 1 < n_pairs:
                s_next = scores(jp + 1)
            ls = slice(jp * LANES, (jp + 1) * LANES)
            v2 = vwin[pl.ds(r0, kw), ls]
            s = s + bias_ref[jp]
            if masked:
                s = jnp.where(valid, s, NEG_BIG)
            m = jnp.max(s, axis=-1, keepdims=True)
            e = jnp.exp(s - m)
            l = jnp.sum(e, axis=-1, keepdims=True)
            o = jnp.dot(e.astype(BF16), v2, preferred_element_type=F32) / l
            o_ref[0, pl.ds(r0, sub), ls] = jnp.where(first_head, o[:sub], o[sub:])

    def run(masked):
        def body(i, carry):
            sub_block(i, masked)
            return carry
        lax.fori_loop(0, qb // sub, body, 0)

    if mask_positions:
        pl.when(c == 0)(lambda: run(True))
        pl.when(c > 0)(lambda: run(False))
    else:
        run(False)


def _attention(q, k, v, k_past, v_past, bias, *, qb, sub, mask_positions, heads):
    b, t, d = q.shape
    kw = bias.shape[2]
    bias = bias.reshape(heads // 2, 2 * sub, kw)
    if k_past is None:
        past = qb
        prev_spec = pl.BlockSpec((1, past, d), lambda i, c: (i, jnp.maximum(c - 1, 0), 0))
        k_past, v_past = k, v
    else:
        assert t == qb
        past = k_past.shape[1]
        prev_spec = pl.BlockSpec((1, past, d), lambda i, c: (i, 0, 0))
    cur_spec = pl.BlockSpec((1, qb, d), lambda i, c: (i, c, 0))
    kern = functools.partial(_attn_kernel, past=past, qb=qb, sub=sub, kw=kw, heads=heads,
                             head_dim=d // heads, mask_positions=mask_positions)
    return pl.pallas_call(
        kern,
        grid=(b, t // qb),
        in_specs=[cur_spec, prev_spec, cur_spec, prev_spec, cur_spec, _full(bias.shape)],
        out_specs=pl.BlockSpec((1, qb, d), lambda i, c: (i, c, 0)),
        out_shape=jax.ShapeDtypeStruct((b, t, d), F32),
        scratch_shapes=[pltpu.VMEM((past + qb, d), BF16), pltpu.VMEM((past + qb, d), BF16)],
        compiler_params=_cparams(("parallel", "arbitrary")),
        name="attention",
    )(q, k_past, k, v_past, v, bias)


def _band_bias(table, sub, kw, chunked):
    assert kw - sub == BAND_PAST or not chunked
    p_len = sub + kw - 1
    dist = (kw - sub) + (sub - 1) - jnp.arange(p_len)
    vec = table[jnp.clip(dist, -REL_CLIP, REL_CLIP) + REL_CLIP].T.astype(F32)
    skew = jnp.tile(vec, (1, sub + 1))[:, :sub * (p_len + 1)].reshape(-1, sub, p_len + 1)
    bias = skew[:, ::-1, :kw]
    if chunked:
        s_loc = jnp.arange(kw)[None, :] - (jnp.arange(sub)[:, None] // CHUNK) * CHUNK
        bias = jnp.where(((s_loc >= 0) & (s_loc < BAND_PAST + CHUNK))[None], bias, NEG_BIG)
    return bias


def _layer_norm(r, g, b):
    mu = jnp.mean(r, axis=-1, keepdims=True)
    d = r - mu
    var = jnp.mean(d * d, axis=-1, keepdims=True)
    return d * lax.rsqrt(var + NORM_EPS) * g + b


def _post_mixer_kernel(y_ref, z_ref, o_ref, x_ref, wo_ref, gs_ref, ga_ref, lg_ref, lb_ref, rwt_ref, rb_ref,
                       hx_ref, cscr, *, alpha, n_experts, d_ssm):
    u = y_ref[...] * _silu(z_ref[...])
    ys = u * lax.rsqrt(jnp.mean(u * u, axis=-1, keepdims=True) + NORM_EPS) * gs_ref[...]
    o = o_ref[...]
    oa = o * lax.rsqrt(jnp.mean(o * o, axis=-1, keepdims=True) + NORM_EPS) * ga_ref[...]
    mix = (jnp.dot(ys.astype(BF16), wo_ref[0:d_ssm, :], preferred_element_type=F32)
           + jnp.dot(oa.astype(BF16), wo_ref[d_ssm:, :], preferred_element_type=F32))
    h = _layer_norm(alpha * x_ref[...] + mix, lg_ref[...], lb_ref[...])

    epg = EXPERTS_PER_GROUP
    assert n_experts == N_EXPERT_GROUPS * epg and epg == 4 and N_EXPERT_GROUPS <= SUBLANES
    h0 = h.astype(BF16)
    h1 = (h - h0.astype(F32)).astype(BF16)
    contract_last = (((1,), (1,)), ((), ()))
    lt0 = lax.dot_general(rwt_ref[...], h0, contract_last, preferred_element_type=F32)
    lt1 = lax.dot_general(rwt_ref[...], h1, contract_last, preferred_element_type=F32)
    rr = ROUTER_ROWS
    logits = lt0[0:rr] + lt0[rr:2 * rr] + lt0[2 * rr:3 * rr] + lt1[0:rr] + lt1[rr:2 * rr]
    grp = lax.broadcasted_iota(jnp.int32, (SUBLANES, logits.shape[1]), 0)
    real = grp < N_EXPERT_GROUPS
    lg = [jnp.where(real, logits[SUBLANES * j:SUBLANES * (j + 1)], -jnp.inf) for j in range(epg)]
    mx = jnp.max(functools.reduce(jnp.maximum, lg), axis=0, keepdims=True)
    ex = [jnp.exp(v - mx) for v in lg]
    zsum = jnp.sum(functools.reduce(lambda p, q: p + q, ex), axis=0, keepdims=True)
    score = [e / zsum for e in ex]
    sel = [score[j] + rb_ref[SUBLANES * j:SUBLANES * (j + 1), :] for j in range(epg)]

    def top2_sum(v):
        a, b = jnp.maximum(v[0], v[1]), jnp.minimum(v[0], v[1])
        c, d = jnp.maximum(v[2], v[3]), jnp.minimum(v[2], v[3])
        return jnp.maximum(a, c) + jnp.maximum(jnp.minimum(a, c), jnp.maximum(b, d))

    gscore = jnp.where(real, top2_sum(sel), -jnp.inf)
    gbest = jnp.max(gscore, axis=0, keepdims=True)
    gidx = jnp.min(jnp.where(gscore == gbest, grp, SUBLANES), axis=0, keepdims=True)
    chosen = grp == gidx
    in_sel = [jnp.sum(jnp.where(chosen, v, 0.0), axis=0, keepdims=True) for v in sel]
    in_score = [jnp.sum(jnp.where(chosen, v, 0.0), axis=0, keepdims=True) for v in score]

    def argmax_first(vals, exclude=None):
        bv, bi = None, None
        for j, v in enumerate(vals):
            if exclude is not None:
                v = jnp.where(exclude == j, -jnp.inf, v)
            if bv is None:
                bv, bi = v, jnp.zeros_like(gidx)
            else:
                upd = v > bv
                bv = jnp.where(upd, v, bv)
                bi = jnp.where(upd, j, bi)
        return bi

    j1 = argmax_first(in_sel)
    j2 = argmax_first(in_sel, exclude=j1)
    s1 = functools.reduce(lambda p, q: p + q, [jnp.where(j1 == j, in_score[j], 0.0) for j in range(epg)])
    s2 = functools.reduce(lambda p, q: p + q, [jnp.where(j2 == j, in_score[j], 0.0) for j in range(epg)])
    g1, g2 = s1 / (s1 + s2), s2 / (s1 + s2)
    first_lo = j1 < j2
    lo, hi = jnp.where(first_lo, j1, j2), jnp.where(first_lo, j2, j1)
    pair = jnp.where(lo == 0, hi - 1, jnp.where(lo == 1, hi + 1, PAIRS_PER_GROUP - 1))
    bucket = gidx * PAIRS_PER_GROUP + pair
    cscr[...] = jnp.zeros_like(cscr)
    cscr[ROUTE_BUCKET:ROUTE_BUCKET + 1, :] = bucket.astype(F32)
    cscr[ROUTE_GATE_LO:ROUTE_GATE_LO + 1, :] = jnp.where(first_lo, g1, g2)
    cscr[ROUTE_GATE_HI:ROUTE_GATE_HI + 1, :] = jnp.where(first_lo, g2, g1)
    d = h.shape[1]
    hx_ref[:, :d] = h
    hx_ref[:, d:] = cscr[...].T


def _post_mixer(y, z, o, x, p, tm, alpha):
    n, d = x.shape
    d_ssm = y.shape[1]
    row = lambda width: pl.BlockSpec((tm, width), lambda i: (i, 0))
    n_experts = p['n_experts']
    kern = functools.partial(_post_mixer_kernel, alpha=alpha, n_experts=n_experts, d_ssm=d_ssm)
    return pl.pallas_call(
        kern,
        grid=(n // tm,),
        in_specs=[row(d_ssm), row(d_ssm), row(o.shape[1]), row(d), _full(p['w_out'].shape),
                  _full(p['ssd_norm_g'].shape), _full(p['attn_norm_g'].shape), _full(p['ln1_g'].shape),
                  _full(p['ln1_b'].shape), _full(p['router_wt'].shape), _full(p['router_bias'].shape)],
        out_specs=row(d + LANES),
        out_shape=jax.ShapeDtypeStruct((n, d + LANES), F32),
        scratch_shapes=[pltpu.VMEM((LANES, tm), F32)],
        compiler_params=_cparams(("parallel",)),
        name="post_mixer",
    )(y, z, o, x, p['w_out'], p['ssd_norm_g'], p['attn_norm_g'], p['ln1_g'], p['ln1_b'], p['router_wt'],
      p['router_bias'])


def _start_row_gather(idx_ref, tile, src_hbm, buf, sem, slot, rows):
    for r in range(rows):
        row = idx_ref[tile, r]
        pltpu.make_async_copy(src_hbm.at[pl.ds(row, 1), :], buf.at[slot, pl.ds(r, 1), :], sem.at[slot]).start()


def _wait_row_gather(src_hbm, buf, sem, slot, rows):
    pltpu.make_async_copy(src_hbm.at[pl.ds(0, rows), :], buf.at[slot], sem.at[slot]).wait()


def _prefetched_rows(idx_ref, src_hbm, buf, sem, rows):
    i = pl.program_id(0)
    slot = lax.rem(i, 2)

    @pl.when(i == 0)
    def _():
        _start_row_gather(idx_ref, 0, src_hbm, buf, sem, 0, rows)

    @pl.when(i + 1 < pl.num_programs(0))
    def _():
        _start_row_gather(idx_ref, i + 1, src_hbm, buf, sem, 1 - slot, rows)

    _wait_row_gather(src_hbm, buf, sem, slot, rows)
    return slot


def _moe_kernel(src_ref, elo_ref, ehi_ref, nused_ref, hx_hbm, wg_lo, wu_lo, wd_lo, wg_hi, wu_hi, wd_hi,
                y_ref, xbuf, sem, *, tm, d):
    del elo_ref, ehi_ref
    slot = _prefetched_rows(src_ref, hx_hbm, xbuf, sem, tm)

    @pl.when(pl.program_id(0) < nused_ref[0])
    def _():
        x = xbuf[slot]
        xb = x[:, :d].astype(BF16)

        def ffn(wg, wu, wd):
            gate = jnp.dot(xb, wg[0], preferred_element_type=F32)
            up = jnp.dot(xb, wu[0], preferred_element_type=F32)
            return jnp.dot((_silu(gate) * up).astype(BF16), wd[0], preferred_element_type=F32)

        y_ref[...] = (x[:, d + ROUTE_GATE_LO:d + ROUTE_GATE_LO + 1] * ffn(wg_lo, wu_lo, wd_lo)
                      + x[:, d + ROUTE_GATE_HI:d + ROUTE_GATE_HI + 1] * ffn(wg_hi, wu_hi, wd_hi))

    @pl.when(pl.program_id(0) >= nused_ref[0])
    def _():
        y_ref[...] = jnp.zeros_like(y_ref)


def _moe(hx, src2d, elo, ehi, nused, p, d):
    n_tiles, tm = src2d.shape
    _, _, d_exp = p['w_gate'].shape
    wspec = lambda sel: pl.BlockSpec((1, d, d_exp), lambda i, s, lo, hi, nu: ((lo if sel == 0 else hi)[i], 0, 0))
    wdspec = lambda sel: pl.BlockSpec((1, d_exp, d), lambda i, s, lo, hi, nu: ((lo if sel == 0 else hi)[i], 0, 0))
    return pl.pallas_call(
        functools.partial(_moe_kernel, tm=tm, d=d),
        grid_spec=pltpu.PrefetchScalarGridSpec(
            num_scalar_prefetch=4,
            grid=(n_tiles,),
            in_specs=[pl.BlockSpec(memory_space=pl.ANY), wspec(0), wspec(0), wdspec(0),
                      wspec(1), wspec(1), wdspec(1)],
            out_specs=pl.BlockSpec((tm, d), lambda i, *_: (i, 0)),
            scratch_shapes=[pltpu.VMEM((2, tm, hx.shape[1]), F32), pltpu.SemaphoreType.DMA((2,))]),
        out_shape=jax.ShapeDtypeStruct((n_tiles * tm, d), F32),
        compiler_params=_cparams(("arbitrary",)),
        name="moe",
    )(src2d, elo, ehi, nused, hx, p['w_gate'], p['w_up'], p['w_down'], p['w_gate'], p['w_up'], p['w_down'])


def _finalize_kernel(pos_ref, y_hbm, hx_ref, lg_ref, lb_ref, out_ref, ybuf, sem, *, tm, d, alpha):
    slot = _prefetched_rows(pos_ref, y_hbm, ybuf, sem, tm)
    out_ref[...] = _layer_norm(alpha * hx_ref[:, :d] + ybuf[slot], lg_ref[...], lb_ref[...])


def _finalize(y_sorted, pos2d, hx, p, d, alpha):
    n_tiles, tm = pos2d.shape
    return pl.pallas_call(
        functools.partial(_finalize_kernel, tm=tm, d=d, alpha=alpha),
        grid_spec=pltpu.PrefetchScalarGridSpec(
            num_scalar_prefetch=1,
            grid=(n_tiles,),
            in_specs=[pl.BlockSpec(memory_space=pl.ANY),
                      pl.BlockSpec((tm, hx.shape[1]), lambda i, pos: (i, 0)),
                      pl.BlockSpec(p['ln2_g'].shape, lambda i, pos: (0, 0)),
                      pl.BlockSpec(p['ln2_b'].shape, lambda i, pos: (0, 0))],
            out_specs=pl.BlockSpec((tm, d), lambda i, pos: (i, 0)),
            scratch_shapes=[pltpu.VMEM((2, tm, d), F32), pltpu.SemaphoreType.DMA((2,))]),
        out_shape=jax.ShapeDtypeStruct((n_tiles * tm, d), F32),
        compiler_params=_cparams(("arbitrary",)),
        name="finalize",
    )(pos2d, y_sorted, hx, p['ln2_g'], p['ln2_b'])


def _route_tables(bucket, n_groups, tm):
    n = bucket.shape[0]
    n_buckets = n_groups * PAIRS_PER_GROUP
    n_tiles = n // tm + n_buckets
    onehot = bucket.reshape(n // tm, tm)[:, :, None] == jnp.arange(n_buckets, dtype=jnp.int32)
    tril = jnp.tril(jnp.ones((tm, tm), BF16))
    within = jnp.einsum('ij,tjb->tib', tril, onehot.astype(BF16), preferred_element_type=F32)
    block_counts = within[:, -1, :]
    before = jnp.cumsum(block_counts, axis=0) - block_counts
    counts = (before[-1] + block_counts[-1]).astype(jnp.int32)
    tiles_per = (counts + tm - 1) // tm
    tile_end = jnp.cumsum(tiles_per)
    start = ((tile_end - tiles_per) * tm).astype(F32)
    pos = jnp.sum(jnp.where(onehot, within - 1.0 + before[:, None, :] + start, 0.0), axis=-1)
    pos = pos.reshape(n).astype(jnp.int32)
    src = jnp.zeros((n_tiles * tm,), jnp.int32).at[pos].set(jnp.arange(n, dtype=jnp.int32), unique_indices=True)
    tile_bucket = jnp.minimum(jnp.searchsorted(tile_end, jnp.arange(n_tiles, dtype=jnp.int32), side='right'),
                              n_buckets - 1).astype(jnp.int32)
    group, pair = tile_bucket // PAIRS_PER_GROUP, tile_bucket % PAIRS_PER_GROUP
    elo = group * EXPERTS_PER_GROUP + jnp.asarray([lo for lo, _ in PAIRS], jnp.int32)[pair]
    ehi = group * EXPERTS_PER_GROUP + jnp.asarray([hi for _, hi in PAIRS], jnp.int32)[pair]
    return pos, src.reshape(n_tiles, tm), elo, ehi, tile_end[-1:].astype(jnp.int32)


def _pick(n, candidates):
    for c in candidates:
        if n % c == 0:
            return c
    raise ValueError(f"no tile size for {n}")


def _router_rows(per_expert):
    a = per_expert.reshape(N_EXPERT_GROUPS, EXPERTS_PER_GROUP, -1).swapaxes(0, 1)
    a = jnp.pad(a, ((0, 0), (0, SUBLANES - N_EXPERT_GROUPS), (0, 0)))
    return a.reshape(ROUTER_ROWS, -1)


def _layer_params(i, w_in, conv_w, conv_b, dt_bias, a_log, d_skip, ssd_norm_g, attn_norm_g, w_out,
                  ln1_g, ln1_b, router_w, router_bias, w_gate, w_up, w_down, ln2_g, ln2_b, dims):
    d_ssm, d_conv, heads, d_att, att_heads = dims
    o = 0
    wi = w_in[i]
    wz = wi[:, o:o + d_ssm]; o += d_ssm
    wxbc = wi[:, o:o + d_conv]; o += d_conv
    wdt = wi[:, o:o + heads]; o += heads
    wq = wi[:, o:o + d_att]; o += d_att
    wk = wi[:, o:o + d_att]; o += d_att
    wv = wi[:, o:o + d_att]
    pad_lanes = lambda v: jnp.pad(v, ((0, 0), (0, LANES - v.shape[1])))
    head_dim = d_ssm // heads
    expand = (jnp.arange(LANES)[:, None] == (jnp.arange(d_ssm)[None, :] // head_dim)).astype(BF16)
    return {
        'wz': wz.astype(BF16), 'wxbc': wxbc.astype(BF16), 'wdt': pad_lanes(wdt).astype(BF16),
        'wq': wq.astype(BF16), 'wk': wk.astype(BF16), 'wv': wv.astype(BF16),
        'q_scale': float(d_att // att_heads) ** -0.5,
        'conv_w': conv_w[i], 'conv_b': conv_b[i][None, :],
        'dt_bias': pad_lanes(dt_bias[i][None, :]), 'a_log': pad_lanes(a_log[i][None, :]),
        'dskip_e': jnp.repeat(d_skip[i], head_dim)[None, :], 'expand': expand, 'a_log_heads': heads,
        'ssd_norm_g': ssd_norm_g[i][None, :], 'attn_norm_g': attn_norm_g[i][None, :],
        'w_out': w_out[i].astype(BF16), 'ln1_g': ln1_g[i][None, :], 'ln1_b': ln1_b[i][None, :],
        'router_wt': jnp.pad(jnp.concatenate([_router_rows(piece) for piece in _split3(router_w.T)]),
                             ((0, LANES - 3 * ROUTER_ROWS), (0, 0))),
        'router_bias': _router_rows(router_bias[:, None]), 'n_experts': router_bias.shape[0],
        'w_gate': w_gate[i].astype(BF16), 'w_up': w_up[i].astype(BF16), 'w_down': w_down[i].astype(BF16),
        'ln2_g': ln2_g[i][None, :], 'ln2_b': ln2_b[i][None, :],
    }


def _trunk(x, layers, biases, conv_prev, h0, k_prev, v_prev, alpha, heads):
    b, t, d = x.shape
    n = b * t
    prompt = k_prev is None
    ks, vs, hs, cs = [], [], [], []
    x2 = x.reshape(n, d)
    tm = _pick(n, (256, 128, 64, 32, 16, 8))
    tm_fin = _pick(n, (512, 256))
    assert n % MOE_TILE == 0
    for i, p in enumerate(layers):
        z, xbc, dt, q, k, v = _in_proj(x2, p, tm)
        d_conv = xbc.shape[1]
        d_ssm = z.shape[1]
        d_state = (d_conv - d_ssm) // (2 * SSM_GROUPS)
        xbc3 = xbc.reshape(b, t, d_conv)
        cprev8 = jnp.pad(conv_prev[i], ((0, 0), (SUBLANES - (CONV_W - 1), 0), (0, 0)))
        h0t = jnp.swapaxes(h0[i].reshape(b, SSM_GROUPS, d_ssm // SSM_GROUPS, d_state), 2, 3)
        blk = _pick(t, (256, 128, 64, 32, 16, 8))
        y, ht = _ssd(xbc3, dt.reshape(b, t, LANES), cprev8, h0t, p, blk)
        q3, k3, v3 = (a.reshape(b, t, -1) for a in (q, k, v))
        if prompt:
            o = _attention(q3, k3, v3, None, None, biases[i], qb=BAND_PAST, sub=ATT_SUB,
                           mask_positions=True, heads=heads)
            keep = min(BAND_PAST, t)
            k_state, v_state = k3[:, t - keep:], v3[:, t - keep:]
        else:
            o = _attention(q3, k3, v3, k_prev[i], v_prev[i], biases[i], qb=t, sub=t,
                           mask_positions=False, heads=heads)
            k_state, v_state = k3, v3
        hx = _post_mixer(y.reshape(n, d_ssm), z, o.reshape(n, -1), x2, p, tm, alpha)
        bucket = hx[:, d + ROUTE_BUCKET].astype(jnp.int32)
        pos, src2d, elo, ehi, nused = _route_tables(bucket, N_EXPERT_GROUPS, MOE_TILE)
        y_sorted = _moe(hx, src2d, elo, ehi, nused, p, d)
        x2 = _finalize(y_sorted, pos.reshape(n // tm_fin, tm_fin), hx, p, d, alpha)
        head_dim = k3.shape[2] // heads
        ks.append(k_state.astype(F32).reshape(b, -1, heads, head_dim))
        vs.append(v_state.astype(F32).reshape(b, -1, heads, head_dim))
        hs.append(jnp.swapaxes(ht, 2, 3).reshape(h0[i].shape))
        full = jnp.concatenate([conv_prev[i], xbc3], axis=1) if t < CONV_W - 1 else xbc3
        cs.append(full[:, full.shape[1] - (CONV_W - 1):])
    return x2.reshape(b, t, d), jnp.stack(ks), jnp.stack(vs), jnp.stack(hs), jnp.stack(cs)


def kernel(x_prompt, x_sample, cache_k, cache_v, state_ssm, state_conv, w_in, conv_w, conv_b, dt_bias, a_log,
           d_skip, ssd_norm_g, attn_norm_g, rel_bias, w_out, ln1_g, ln1_b, router_w, router_bias, w_gate, w_up,
           w_down, ln2_g, ln2_b):
    depth = w_in.shape[0]
    heads = cache_k.shape[3]
    d_att = heads * cache_k.shape[4]
    d_conv = state_conv.shape[3]
    ssm_heads = state_ssm.shape[2]
    d_ssm = ssm_heads * state_ssm.shape[3]
    alpha = (2 * depth) ** 0.25
    dims = (d_ssm, d_conv, ssm_heads, d_att, heads)
    layers = [_layer_params(i, w_in, conv_w, conv_b, dt_bias, a_log, d_skip, ssd_norm_g, attn_norm_g, w_out,
                            ln1_g, ln1_b, router_w, router_bias, w_gate, w_up, w_down, ln2_g, ln2_b, dims)
              for i in range(depth)]
    bp, tp, _ = x_prompt.shape
    bs, ts, _ = x_sample.shape

    conv0 = jnp.zeros((depth, bp, CONV_W - 1, d_conv), F32)
    h00 = jnp.zeros((depth, bp) + state_ssm.shape[2:], F32)
    bias_p = [_band_bias(rel_bias[i], ATT_SUB, BAND_PAST + ATT_SUB, True) for i in range(depth)]
    y_p, k_p, v_p, h_p, c_p = _trunk(x_prompt, layers, bias_p, conv0, h00, None, None, alpha, heads)

    past = cache_k.shape[2]
    bias_s = [_band_bias(rel_bias[i], ts, past + ts, False) for i in range(depth)]
    ck = cache_k.reshape(depth, bs, past, d_att).astype(BF16)
    cv = cache_v.reshape(depth, bs, past, d_att).astype(BF16)
    y_s, k_s, v_s, h_s, c_s = _trunk(x_sample, layers, bias_s, state_conv, state_ssm, ck, cv, alpha, heads)
    return (y_p, y_s, k_p, v_p, h_p, c_p, k_s, v_s, h_s, c_s)
```

```python
import functools

import jax
import jax.numpy as jnp
from jax import lax
from jax.experimental import pallas as pl
from jax.experimental.pallas import tpu as pltpu

F32 = jnp.float32
BF16 = jnp.bfloat16

CHUNK = 64
BAND_PREV_CHUNKS = 8
BAND_PAST = BAND_PREV_CHUNKS * CHUNK
REL_CLIP = 256
SSM_GROUPS = 2
CONV_W = 4
N_EXPERT_GROUPS = 4
EXPERTS_PER_GROUP = 4
PAIRS = [(lo, hi) for lo in range(EXPERTS_PER_GROUP) for hi in range(lo + 1, EXPERTS_PER_GROUP)]
PAIRS_PER_GROUP = len(PAIRS)
ROUTE_BUCKET, ROUTE_GATE_LO, ROUTE_GATE_HI = 0, 1, 2
MOE_TILE = 256
ATT_SUB = 4 * CHUNK
SSD_SUB = 128
NORM_EPS = 1e-5
NEG_BIG = -1e30

LANES = 128
SUBLANES = 8
VMEM_LIMIT = 56 * 1024 * 1024
ROUTER_ROWS = SUBLANES * EXPERTS_PER_GROUP


def _cparams(sem):
    return pltpu.CompilerParams(dimension_semantics=sem, vmem_limit_bytes=VMEM_LIMIT)


def _silu(v):
    half = 0.5 * v
    return half + half * jnp.tanh(half)


def _split3(v):
    p1 = v.astype(BF16)
    r1 = v - p1.astype(F32)
    p2 = r1.astype(BF16)
    p3 = (r1 - p2.astype(F32)).astype(BF16)
    return p1, p2, p3


def _full(shape):
    return pl.BlockSpec(shape, lambda *_: (0,) * len(shape))


def _in_proj_kernel(x_ref, wz_ref, wxbc_ref, wdt_ref, wq_ref, wk_ref, wv_ref,
                    z_ref, xbc_ref, dt_ref, q_ref, k_ref, v_ref, *, q_scale):
    xb = x_ref[...].astype(BF16)
    z_ref[...] = jnp.dot(xb, wz_ref[...], preferred_element_type=F32)
    xbc_ref[...] = jnp.dot(xb, wxbc_ref[...], preferred_element_type=F32)
    dt_ref[...] = jnp.dot(xb, wdt_ref[...], preferred_element_type=F32)
    q_ref[...] = (jnp.dot(xb, wq_ref[...], preferred_element_type=F32) * q_scale).astype(BF16)
    k_ref[...] = jnp.dot(xb, wk_ref[...], preferred_element_type=F32).astype(BF16)
    v_ref[...] = jnp.dot(xb, wv_ref[...], preferred_element_type=F32).astype(BF16)


def _in_proj(x2d, w, tm):
    n, d = x2d.shape
    d_ssm, d_conv, d_att = w['wz'].shape[1], w['wxbc'].shape[1], w['wq'].shape[1]
    row = lambda width: pl.BlockSpec((tm, width), lambda i: (i, 0))
    return pl.pallas_call(
        functools.partial(_in_proj_kernel, q_scale=w['q_scale']),
        grid=(n // tm,),
        in_specs=[row(d), _full(w['wz'].shape), _full(w['wxbc'].shape), _full(w['wdt'].shape),
                  _full(w['wq'].shape), _full(w['wk'].shape), _full(w['wv'].shape)],
        out_specs=[row(d_ssm), row(d_conv), row(LANES), row(d_att), row(d_att), row(d_att)],
        out_shape=[jax.ShapeDtypeStruct((n, d_ssm), F32), jax.ShapeDtypeStruct((n, d_conv), F32),
                   jax.ShapeDtypeStruct((n, LANES), F32), jax.ShapeDtypeStruct((n, d_att), BF16),
                   jax.ShapeDtypeStruct((n, d_att), BF16), jax.ShapeDtypeStruct((n, d_att), BF16)],
        compiler_params=_cparams(("parallel",)),
        name="in_proj",
    )(x2d, w['wz'], w['wxbc'], w['wdt'], w['wq'], w['wk'], w['wv'])


def _ssd_kernel(xbc_ref, dt_ref, cprev_ref, h0_ref, convw_ref, convb_ref, dtb_ref, alog_ref,
                dskip_ref, tril_ref, ex_ref, y_ref, hout_ref, cbuf, hst, xc_s, dt_s,
                *, blk, sub, d_ssm, d_state, head_dim):
    c = pl.program_id(1)
    gw = d_ssm // SSM_GROUPS

    @pl.when(c == 0)
    def _():
        hst[...] = h0_ref[0]
        cbuf[0:SUBLANES, :] = cprev_ref[0]

    cbuf[SUBLANES:SUBLANES + blk, :] = xbc_ref[0]
    acc = jnp.broadcast_to(convb_ref[...], (blk, convb_ref.shape[1]))
    for i in range(CONV_W):
        lo = SUBLANES - (CONV_W - 1) + i
        acc = acc + cbuf[lo:lo + blk, :] * convw_ref[i:i + 1, :]
    cbuf[0:SUBLANES, :] = cbuf[blk:blk + SUBLANES, :]
    xc_s[...] = _silu(acc)

    dtr = dt_ref[0] + dtb_ref[...]
    dt_s[...] = jnp.maximum(dtr, 0.0) + jnp.log1p(jnp.exp(-jnp.abs(dtr)))
    a = -jnp.exp(alog_ref[...])

    row = lax.broadcasted_iota(jnp.int32, (sub, sub), 0)
    col = lax.broadcasted_iota(jnp.int32, (sub, sub), 1)
    causal = row >= col
    lane = lax.broadcasted_iota(jnp.int32, (sub, LANES), 1)
    first_head = lane < head_dim
    assert 2 * head_dim == LANES
    pairs_per_group = gw // LANES

    for sc in range(blk // sub):
        r0 = sc * sub
        xs = xc_s[r0:r0 + sub, :d_ssm]
        dt = dt_s[r0:r0 + sub, :]
        acs = sum(jnp.dot(tril_ref[...], piece, preferred_element_type=F32) for piece in _split3(dt * a))
        acs_t = acs.T
        acs_e = sum(jnp.dot(piece, ex_ref[...], preferred_element_type=F32) for piece in _split3(acs))
        dt_e = sum(jnp.dot(piece, ex_ref[...], preferred_element_type=F32) for piece in _split3(dt))
        last_e = acs_e[sub - 1:sub, :]
        dtx = dt_e * xs
        xw = (jnp.exp(last_e - acs_e) * dtx).astype(BF16)
        dtx = dtx.astype(BF16)
        eacs_e = jnp.exp(acs_e)
        cdec_e = jnp.exp(last_e)
        for g in range(SSM_GROUPS):
            b0 = d_ssm + g * d_state
            c0 = d_ssm + (SSM_GROUPS + g) * d_state
            bm = xc_s[r0:r0 + sub, b0:b0 + d_state].astype(BF16)
            cm = xc_s[r0:r0 + sub, c0:c0 + d_state].astype(BF16)
            hg = hst[g]
            cb = lax.dot_general(cm, bm, (((1,), (1,)), ((), ())), preferred_element_type=F32)
            y_off = jnp.dot(cm, hg.astype(BF16), preferred_element_type=F32)
            for jp in range(pairs_per_group):
                p0 = g * gw + jp * LANES
                h0 = p0 // head_dim
                xpair = dtx[:, p0:p0 + LANES]
                outs = []
                for hh in (h0, h0 + 1):
                    seg = acs[:, hh:hh + 1] - acs_t[hh:hh + 1, :]
                    dec = jnp.exp(jnp.where(causal, seg, NEG_BIG))
                    outs.append(jnp.dot((cb * dec).astype(BF16), xpair, preferred_element_type=F32))
                y_diag = jnp.where(first_head, outs[0], outs[1])
                sl = slice(p0, p0 + LANES)
                y_ref[0, r0:r0 + sub, sl] = (y_diag + y_off[:, jp * LANES:(jp + 1) * LANES] * eacs_e[:, sl]
                                             + dskip_ref[:, sl] * xs[:, sl])
            st = lax.dot_general(bm, xw[:, g * gw:(g + 1) * gw], (((0,), (0,)), ((), ())),
                                 preferred_element_type=F32)
            hst[g] = hg * cdec_e[:, g * gw:(g + 1) * gw] + st

    @pl.when(c == pl.num_programs(1) - 1)
    def _():
        hout_ref[0] = hst[...]


def _ssd(xbc, dt, cprev8, h0t, p, blk):
    b, t, d_conv = xbc.shape
    d_ssm = p['dskip_e'].shape[1]
    d_state = (d_conv - d_ssm) // (2 * SSM_GROUPS)
    gw = d_ssm // SSM_GROUPS
    sub = min(blk, SSD_SUB)
    tril = jnp.tril(jnp.ones((sub, sub), BF16))
    kern = functools.partial(_ssd_kernel, blk=blk, sub=sub, d_ssm=d_ssm, d_state=d_state,
                             head_dim=d_ssm // p['a_log_heads'])
    return pl.pallas_call(
        kern,
        grid=(b, t // blk),
        in_specs=[pl.BlockSpec((1, blk, d_conv), lambda i, c: (i, c, 0)),
                  pl.BlockSpec((1, blk, LANES), lambda i, c: (i, c, 0)),
                  pl.BlockSpec((1, SUBLANES, d_conv), lambda i, c: (i, 0, 0)),
                  pl.BlockSpec((1, SSM_GROUPS, d_state, gw), lambda i, c: (i, 0, 0, 0)),
                  _full(p['conv_w'].shape), _full(p['conv_b'].shape), _full(p['dt_bias'].shape),
                  _full(p['a_log'].shape), _full(p['dskip_e'].shape), _full(tril.shape),
                  _full(p['expand'].shape)],
        out_specs=[pl.BlockSpec((1, blk, d_ssm), lambda i, c: (i, c, 0)),
                   pl.BlockSpec((1, SSM_GROUPS, d_state, gw), lambda i, c: (i, 0, 0, 0))],
        out_shape=[jax.ShapeDtypeStruct((b, t, d_ssm), F32),
                   jax.ShapeDtypeStruct((b, SSM_GROUPS, d_state, gw), F32)],
        scratch_shapes=[pltpu.VMEM((blk + SUBLANES, d_conv), F32),
                        pltpu.VMEM((SSM_GROUPS, d_state, gw), F32),
                        pltpu.VMEM((blk, d_conv), F32), pltpu.VMEM((blk, LANES), F32)],
        compiler_params=_cparams(("parallel", "arbitrary")),
        name="ssd",
    )(xbc, dt, cprev8, h0t, p['conv_w'], p['conv_b'], p['dt_bias'], p['a_log'], p['dskip_e'], tril,
      p['expand'])


def _attn_kernel(q_ref, kp_ref, kc_ref, vp_ref, vc_ref, bias_ref, o_ref, kwin, vwin,
                 *, past, qb, sub, kw, heads, head_dim, mask_positions):
    c = pl.program_id(1)
    kwin[0:past, :] = kp_ref[0]
    kwin[past:past + qb, :] = kc_ref[0]
    vwin[0:past, :] = vp_ref[0]
    vwin[past:past + qb, :] = vc_ref[0]
    lane = lax.broadcasted_iota(jnp.int32, (sub, LANES), 1)
    first_head = lane < head_dim

    def sub_block(i, masked):
        r0 = pl.multiple_of(i * sub, sub)
        if masked:
            col = lax.broadcasted_iota(jnp.int32, (2 * sub, kw), 1)
            valid = (col + (c * qb - past + i * sub)) >= 0
        n_pairs = heads * head_dim // LANES

        def scores(jp):
            ls = slice(jp * LANES, (jp + 1) * LANES)
            q2 = q_ref[0, pl.ds(r0, sub), ls]
            k2 = kwin[pl.ds(r0, kw), ls]
            zero = jnp.zeros_like(q2)
            qq = jnp.concatenate([jnp.where(first_head, q2, zero), jnp.where(first_head, zero, q2)], axis=0)
            return lax.dot_general(qq, k2, (((1,), (1,)), ((), ())), preferred_element_type=F32)

        s_next = scores(0)
        for jp in range(n_pairs):
            s = s_next
            if jp + 1 < n_pairs:
                s_next = scores(jp + 1)
            ls = slice(jp * LANES, (jp + 1) * LANES)
            v2 = vwin[pl.ds(r0, kw), ls]
            s = s + bias_ref[jp]
            if masked:
                s = jnp.where(valid, s, NEG_BIG)
            m = jnp.max(s, axis=-1, keepdims=True)
            e = jnp.exp(s - m)
            l = jnp.sum(e, axis=-1, keepdims=True)
            o = jnp.dot(e.astype(BF16), v2, preferred_element_type=F32) / l
            o_ref[0, pl.ds(r0, sub), ls] = jnp.where(first_head, o[:sub], o[sub:])

    def run(masked):
        def body(i, carry):
            sub_block(i, masked)
            return carry
        lax.fori_loop(0, qb // sub, body, 0)

    if mask_positions:
        pl.when(c == 0)(lambda: run(True))
        pl.when(c > 0)(lambda: run(False))
    else:
        run(False)


def _attention(q, k, v, k_past, v_past, bias, *, qb, sub, mask_positions, heads):
    b, t, d = q.shape
    kw = bias.shape[2]
    bias = bias.reshape(heads // 2, 2 * sub, kw)
    if k_past is None:
        past = qb
        prev_spec = pl.BlockSpec((1, past, d), lambda i, c: (i, jnp.maximum(c - 1, 0), 0))
        k_past, v_past = k, v
    else:
        assert t == qb
        past = k_past.shape[1]
        prev_spec = pl.BlockSpec((1, past, d), lambda i, c: (i, 0, 0))
    cur_spec = pl.BlockSpec((1, qb, d), lambda i, c: (i, c, 0))
    kern = functools.partial(_attn_kernel, past=past, qb=qb, sub=sub, kw=kw, heads=heads,
                             head_dim=d // heads, mask_positions=mask_positions)
    return pl.pallas_call(
        kern,
        grid=(b, t // qb),
        in_specs=[cur_spec, prev_spec, cur_spec, prev_spec, cur_spec, _full(bias.shape)],
        out_specs=pl.BlockSpec((1, qb, d), lambda i, c: (i, c, 0)),
        out_shape=jax.ShapeDtypeStruct((b, t, d), F32),
        scratch_shapes=[pltpu.VMEM((past + qb, d), BF16), pltpu.VMEM((past + qb, d), BF16)],
        compiler_params=_cparams(("parallel", "arbitrary")),
        name="attention",
    )(q, k_past, k, v_past, v, bias)


def _band_bias(table, sub, kw, chunked):
    assert kw - sub == BAND_PAST or not chunked
    p_len = sub + kw
    dist = (kw - sub) + (sub - 1) - jnp.arange(p_len)
    vec = table[jnp.clip(dist, -REL_CLIP, REL_CLIP) + REL_CLIP].T.astype(F32)
    skew = jnp.tile(vec, (1, sub))[:, :sub * (p_len - 1)].reshape(-1, sub, p_len - 1)
    bias = skew[:, :, sub - 1:sub - 1 + kw]
    if chunked:
        s_loc = jnp.arange(kw)[None, :] - (jnp.arange(sub)[:, None] // CHUNK) * CHUNK
        bias = jnp.where(((s_loc >= 0) & (s_loc < BAND_PAST + CHUNK))[None], bias, NEG_BIG)
    return bias


def _layer_norm(r, g, b):
    mu = jnp.mean(r, axis=-1, keepdims=True)
    d = r - mu
    var = jnp.mean(d * d, axis=-1, keepdims=True)
    return d * lax.rsqrt(var + NORM_EPS) * g + b


def _post_mixer_kernel(y_ref, z_ref, o_ref, x_ref, wo_ref, gs_ref, ga_ref, lg_ref, lb_ref, rwt_ref, rb_ref,
                       hx_ref, cscr, *, alpha, n_experts, d_ssm):
    u = y_ref[...] * _silu(z_ref[...])
    ys = u * lax.rsqrt(jnp.mean(u * u, axis=-1, keepdims=True) + NORM_EPS) * gs_ref[...]
    o = o_ref[...]
    oa = o * lax.rsqrt(jnp.mean(o * o, axis=-1, keepdims=True) + NORM_EPS) * ga_ref[...]
    mix = (jnp.dot(ys.astype(BF16), wo_ref[0:d_ssm, :], preferred_element_type=F32)
           + jnp.dot(oa.astype(BF16), wo_ref[d_ssm:, :], preferred_element_type=F32))
    h = _layer_norm(alpha * x_ref[...] + mix, lg_ref[...], lb_ref[...])

    epg = EXPERTS_PER_GROUP
    assert n_experts == N_EXPERT_GROUPS * epg and epg == 4 and N_EXPERT_GROUPS <= SUBLANES
    h0 = h.astype(BF16)
    h1 = (h - h0.astype(F32)).astype(BF16)
    contract_last = (((1,), (1,)), ((), ()))
    lt0 = lax.dot_general(rwt_ref[...], h0, contract_last, preferred_element_type=F32)
    lt1 = lax.dot_general(rwt_ref[...], h1, contract_last, preferred_element_type=F32)
    rr = ROUTER_ROWS
    logits = lt0[0:rr] + lt0[rr:2 * rr] + lt0[2 * rr:3 * rr] + lt1[0:rr] + lt1[rr:2 * rr]
    grp = lax.broadcasted_iota(jnp.int32, (SUBLANES, logits.shape[1]), 0)
    real = grp < N_EXPERT_GROUPS
    lg = [jnp.where(real, logits[SUBLANES * j:SUBLANES * (j + 1)], -jnp.inf) for j in range(epg)]
    mx = jnp.max(functools.reduce(jnp.maximum, lg), axis=0, keepdims=True)
    ex = [jnp.exp(v - mx) for v in lg]
    zsum = jnp.sum(functools.reduce(lambda p, q: p + q, ex), axis=0, keepdims=True)
    score = [e / zsum for e in ex]
    sel = [score[j] + rb_ref[SUBLANES * j:SUBLANES * (j + 1), :] for j in range(epg)]

    def top2_sum(v):
        a, b = jnp.maximum(v[0], v[1]), jnp.minimum(v[0], v[1])
        c, d = jnp.maximum(v[2], v[3]), jnp.minimum(v[2], v[3])
        return jnp.maximum(a, c) + jnp.maximum(jnp.minimum(a, c), jnp.maximum(b, d))

    gscore = jnp.where(real, top2_sum(sel), -jnp.inf)
    gbest = jnp.max(gscore, axis=0, keepdims=True)
    gidx = jnp.min(jnp.where(gscore == gbest, grp, SUBLANES), axis=0, keepdims=True)
    chosen = grp == gidx
    in_sel = [jnp.sum(jnp.where(chosen, v, 0.0), axis=0, keepdims=True) for v in sel]
    in_score = [jnp.sum(jnp.where(chosen, v, 0.0), axis=0, keepdims=True) for v in score]

    def argmax_first(vals, exclude=None):
        bv, bi = None, None
        for j, v in enumerate(vals):
            if exclude is not None:
                v = jnp.where(exclude == j, -jnp.inf, v)
            if bv is None:
                bv, bi = v, jnp.zeros_like(gidx)
            else:
                upd = v > bv
                bv = jnp.where(upd, v, bv)
                bi = jnp.where(upd, j, bi)
        return bi

    j1 = argmax_first(in_sel)
    j2 = argmax_first(in_sel, exclude=j1)
    s1 = functools.reduce(lambda p, q: p + q, [jnp.where(j1 == j, in_score[j], 0.0) for j in range(epg)])
    s2 = functools.reduce(lambda p, q: p + q, [jnp.where(j2 == j, in_score[j], 0.0) for j in range(epg)])
    g1, g2 = s1 / (s1 + s2), s2 / (s1 + s2)
    first_lo = j1 < j2
    lo, hi = jnp.where(first_lo, j1, j2), jnp.where(first_lo, j2, j1)
    pair = jnp.where(lo == 0, hi - 1, jnp.where(lo == 1, hi + 1, PAIRS_PER_GROUP - 1))
    bucket = gidx * PAIRS_PER_GROUP + pair
    cscr[...] = jnp.zeros_like(cscr)
    cscr[ROUTE_BUCKET:ROUTE_BUCKET + 1, :] = bucket.astype(F32)
    cscr[ROUTE_GATE_LO:ROUTE_GATE_LO + 1, :] = jnp.where(first_lo, g1, g2)
    cscr[ROUTE_GATE_HI:ROUTE_GATE_HI + 1, :] = jnp.where(first_lo, g2, g1)
    d = h.shape[1]
    hx_ref[:, :d] = h
    hx_ref[:, d:] = cscr[...].T


def _post_mixer(y, z, o, x, p, tm, alpha):
    n, d = x.shape
    d_ssm = y.shape[1]
    row = lambda width: pl.BlockSpec((tm, width), lambda i: (i, 0))
    n_experts = p['n_experts']
    kern = functools.partial(_post_mixer_kernel, alpha=alpha, n_experts=n_experts, d_ssm=d_ssm)
    return pl.pallas_call(
        kern,
        grid=(n // tm,),
        in_specs=[row(d_ssm), row(d_ssm), row(o.shape[1]), row(d), _full(p['w_out'].shape),
                  _full(p['ssd_norm_g'].shape), _full(p['attn_norm_g'].shape), _full(p['ln1_g'].shape),
                  _full(p['ln1_b'].shape), _full(p['router_wt'].shape), _full(p['router_bias'].shape)],
        out_specs=row(d + LANES),
        out_shape=jax.ShapeDtypeStruct((n, d + LANES), F32),
        scratch_shapes=[pltpu.VMEM((LANES, tm), F32)],
        compiler_params=_cparams(("parallel",)),
        name="post_mixer",
    )(y, z, o, x, p['w_out'], p['ssd_norm_g'], p['attn_norm_g'], p['ln1_g'], p['ln1_b'], p['router_wt'],
      p['router_bias'])


def _start_row_gather(idx_ref, tile, src_hbm, buf, sem, slot, rows):
    for r in range(rows):
        row = idx_ref[tile, r]
        pltpu.make_async_copy(src_hbm.at[pl.ds(row, 1), :], buf.at[slot, pl.ds(r, 1), :], sem.at[slot]).start()


def _wait_row_gather(src_hbm, buf, sem, slot, rows):
    pltpu.make_async_copy(src_hbm.at[pl.ds(0, rows), :], buf.at[slot], sem.at[slot]).wait()


def _prefetched_rows(idx_ref, src_hbm, buf, sem, rows):
    i = pl.program_id(0)
    slot = lax.rem(i, 2)

    @pl.when(i == 0)
    def _():
        _start_row_gather(idx_ref, 0, src_hbm, buf, sem, 0, rows)

    @pl.when(i + 1 < pl.num_programs(0))
    def _():
        _start_row_gather(idx_ref, i + 1, src_hbm, buf, sem, 1 - slot, rows)

    _wait_row_gather(src_hbm, buf, sem, slot, rows)
    return slot


def _moe_kernel(src_ref, elo_ref, ehi_ref, nused_ref, hx_hbm, wg_lo, wu_lo, wd_lo, wg_hi, wu_hi, wd_hi,
                y_ref, xbuf, sem, *, tm, d):
    del elo_ref, ehi_ref
    slot = _prefetched_rows(src_ref, hx_hbm, xbuf, sem, tm)

    @pl.when(pl.program_id(0) < nused_ref[0])
    def _():
        x = xbuf[slot]
        xb = x[:, :d].astype(BF16)

        def ffn(wg, wu, wd):
            gate = jnp.dot(xb, wg[0], preferred_element_type=F32)
            up = jnp.dot(xb, wu[0], preferred_element_type=F32)
            return jnp.dot((_silu(gate) * up).astype(BF16), wd[0], preferred_element_type=F32)

        y_ref[...] = (x[:, d + ROUTE_GATE_LO:d + ROUTE_GATE_LO + 1] * ffn(wg_lo, wu_lo, wd_lo)
                      + x[:, d + ROUTE_GATE_HI:d + ROUTE_GATE_HI + 1] * ffn(wg_hi, wu_hi, wd_hi))

    @pl.when(pl.program_id(0) >= nused_ref[0])
    def _():
        y_ref[...] = jnp.zeros_like(y_ref)


def _moe(hx, src2d, elo, ehi, nused, p, d):
    n_tiles, tm = src2d.shape
    _, _, d_exp = p['w_gate'].shape
    wspec = lambda sel: pl.BlockSpec((1, d, d_exp), lambda i, s, lo, hi, nu: ((lo if sel == 0 else hi)[i], 0, 0))
    wdspec = lambda sel: pl.BlockSpec((1, d_exp, d), lambda i, s, lo, hi, nu: ((lo if sel == 0 else hi)[i], 0, 0))
    return pl.pallas_call(
        functools.partial(_moe_kernel, tm=tm, d=d),
        grid_spec=pltpu.PrefetchScalarGridSpec(
            num_scalar_prefetch=4,
            grid=(n_tiles,),
            in_specs=[pl.BlockSpec(memory_space=pl.ANY), wspec(0), wspec(0), wdspec(0),
                      wspec(1), wspec(1), wdspec(1)],
            out_specs=pl.BlockSpec((tm, d), lambda i, *_: (i, 0)),
            scratch_shapes=[pltpu.VMEM((2, tm, hx.shape[1]), F32), pltpu.SemaphoreType.DMA((2,))]),
        out_shape=jax.ShapeDtypeStruct((n_tiles * tm, d), F32),
        compiler_params=_cparams(("arbitrary",)),
        name="moe",
    )(src2d, elo, ehi, nused, hx, p['w_gate'], p['w_up'], p['w_down'], p['w_gate'], p['w_up'], p['w_down'])


def _finalize_kernel(pos_ref, y_hbm, hx_ref, lg_ref, lb_ref, out_ref, ybuf, sem, *, tm, d, alpha):
    slot = _prefetched_rows(pos_ref, y_hbm, ybuf, sem, tm)
    out_ref[...] = _layer_norm(alpha * hx_ref[:, :d] + ybuf[slot], lg_ref[...], lb_ref[...])


def _finalize(y_sorted, pos2d, hx, p, d, alpha):
    n_tiles, tm = pos2d.shape
    return pl.pallas_call(
        functools.partial(_finalize_kernel, tm=tm, d=d, alpha=alpha),
        grid_spec=pltpu.PrefetchScalarGridSpec(
            num_scalar_prefetch=1,
            grid=(n_tiles,),
            in_specs=[pl.BlockSpec(memory_space=pl.ANY),
                      pl.BlockSpec((tm, hx.shape[1]), lambda i, pos: (i, 0)),
                      pl.BlockSpec(p['ln2_g'].shape, lambda i, pos: (0, 0)),
                      pl.BlockSpec(p['ln2_b'].shape, lambda i, pos: (0, 0))],
            out_specs=pl.BlockSpec((tm, d), lambda i, pos: (i, 0)),
            scratch_shapes=[pltpu.VMEM((2, tm, d), F32), pltpu.SemaphoreType.DMA((2,))]),
        out_shape=jax.ShapeDtypeStruct((n_tiles * tm, d), F32),
        compiler_params=_cparams(("arbitrary",)),
        name="finalize",
    )(pos2d, y_sorted, hx, p['ln2_g'], p['ln2_b'])


def _route_tables(bucket, n_groups, tm):
    n = bucket.shape[0]
    n_buckets = n_groups * PAIRS_PER_GROUP
    n_tiles = n // tm + n_buckets
    onehot = bucket.reshape(n // tm, tm)[:, :, None] == jnp.arange(n_buckets, dtype=jnp.int32)
    tril = jnp.tril(jnp.ones((tm, tm), BF16))
    within = jnp.einsum('ij,tjb->tib', tril, onehot.astype(BF16), preferred_element_type=F32)
    block_counts = within[:, -1, :]
    before = jnp.cumsum(block_counts, axis=0) - block_counts
    counts = (before[-1] + block_counts[-1]).astype(jnp.int32)
    tiles_per = (counts + tm - 1) // tm
    tile_end = jnp.cumsum(tiles_per)
    start = ((tile_end - tiles_per) * tm).astype(F32)
    pos = jnp.sum(jnp.where(onehot, within - 1.0 + before[:, None, :] + start, 0.0), axis=-1)
    pos = pos.reshape(n).astype(jnp.int32)
    src = jnp.zeros((n_tiles * tm,), jnp.int32).at[pos].set(jnp.arange(n, dtype=jnp.int32), unique_indices=True)
    tile_bucket = jnp.sum(tile_end[None, :] <= jnp.arange(n_tiles, dtype=jnp.int32)[:, None], axis=1)
    tile_bucket = jnp.minimum(tile_bucket, n_buckets - 1).astype(jnp.int32)
    group, pair = tile_bucket // PAIRS_PER_GROUP, tile_bucket % PAIRS_PER_GROUP
    elo = group * EXPERTS_PER_GROUP + jnp.asarray([lo for lo, _ in PAIRS], jnp.int32)[pair]
    ehi = group * EXPERTS_PER_GROUP + jnp.asarray([hi for _, hi in PAIRS], jnp.int32)[pair]
    return pos, src.reshape(n_tiles, tm), elo, ehi, tile_end[-1:].astype(jnp.int32)


def _pick(n, candidates):
    for c in candidates:
        if n % c == 0:
            return c
    raise ValueError(f"no tile size for {n}")


def _router_rows(per_expert):
    a = per_expert.reshape(N_EXPERT_GROUPS, EXPERTS_PER_GROUP, -1).swapaxes(0, 1)
    a = jnp.pad(a, ((0, 0), (0, SUBLANES - N_EXPERT_GROUPS), (0, 0)))
    return a.reshape(ROUTER_ROWS, -1)


def _layer_params(i, w_in, conv_w, conv_b, dt_bias, a_log, d_skip, ssd_norm_g, attn_norm_g, w_out,
                  ln1_g, ln1_b, router_w, router_bias, w_gate, w_up, w_down, ln2_g, ln2_b, dims):
    d_ssm, d_conv, heads, d_att, att_heads = dims
    o = 0
    wi = w_in[i]
    wz = wi[:, o:o + d_ssm]; o += d_ssm
    wxbc = wi[:, o:o + d_conv]; o += d_conv
    wdt = wi[:, o:o + heads]; o += heads
    wq = wi[:, o:o + d_att]; o += d_att
    wk = wi[:, o:o + d_att]; o += d_att
    wv = wi[:, o:o + d_att]
    pad_lanes = lambda v: jnp.pad(v, ((0, 0), (0, LANES - v.shape[1])))
    head_dim = d_ssm // heads
    expand = (jnp.arange(LANES)[:, None] == (jnp.arange(d_ssm)[None, :] // head_dim)).astype(BF16)
    return {
        'wz': wz.astype(BF16), 'wxbc': wxbc.astype(BF16), 'wdt': pad_lanes(wdt).astype(BF16),
        'wq': wq.astype(BF16), 'wk': wk.astype(BF16), 'wv': wv.astype(BF16),
        'q_scale': float(d_att // att_heads) ** -0.5,
        'conv_w': conv_w[i], 'conv_b': conv_b[i][None, :],
        'dt_bias': pad_lanes(dt_bias[i][None, :]), 'a_log': pad_lanes(a_log[i][None, :]),
        'dskip_e': jnp.repeat(d_skip[i], head_dim)[None, :], 'expand': expand, 'a_log_heads': heads,
        'ssd_norm_g': ssd_norm_g[i][None, :], 'attn_norm_g': attn_norm_g[i][None, :],
        'w_out': w_out[i].astype(BF16), 'ln1_g': ln1_g[i][None, :], 'ln1_b': ln1_b[i][None, :],
        'router_wt': jnp.pad(jnp.concatenate([_router_rows(piece) for piece in _split3(router_w.T)]),
                             ((0, LANES - 3 * ROUTER_ROWS), (0, 0))),
        'router_bias': _router_rows(router_bias[:, None]), 'n_experts': router_bias.shape[0],
        'w_gate': w_gate[i].astype(BF16), 'w_up': w_up[i].astype(BF16), 'w_down': w_down[i].astype(BF16),
        'ln2_g': ln2_g[i][None, :], 'ln2_b': ln2_b[i][None, :],
    }


def _trunk(x, layers, biases, conv_prev, h0, k_prev, v_prev, alpha, heads):
    b, t, d = x.shape
    n = b * t
    prompt = k_prev is None
    ks, vs, hs, cs = [], [], [], []
    x2 = x.reshape(n, d)
    tm = _pick(n, (256, 128, 64, 32, 16, 8))
    tm_fin = _pick(n, (512, 256))
    assert n % MOE_TILE == 0
    for i, p in enumerate(layers):
        z, xbc, dt, q, k, v = _in_proj(x2, p, tm)
        d_conv = xbc.shape[1]
        d_ssm = z.shape[1]
        d_state = (d_conv - d_ssm) // (2 * SSM_GROUPS)
        xbc3 = xbc.reshape(b, t, d_conv)
        cprev8 = jnp.pad(conv_prev[i], ((0, 0), (SUBLANES - (CONV_W - 1), 0), (0, 0)))
        h0t = jnp.swapaxes(h0[i].reshape(b, SSM_GROUPS, d_ssm // SSM_GROUPS, d_state), 2, 3)
        blk = _pick(t, (256, 128, 64, 32, 16, 8))
        y, ht = _ssd(xbc3, dt.reshape(b, t, LANES), cprev8, h0t, p, blk)
        q3, k3, v3 = (a.reshape(b, t, -1) for a in (q, k, v))
        if prompt:
            o = _attention(q3, k3, v3, None, None, biases[i], qb=BAND_PAST, sub=ATT_SUB,
                           mask_positions=True, heads=heads)
            keep = min(BAND_PAST, t)
            k_state, v_state = k3[:, t - keep:], v3[:, t - keep:]
        else:
            o = _attention(q3, k3, v3, k_prev[i], v_prev[i], biases[i], qb=t, sub=t,
                           mask_positions=False, heads=heads)
            k_state, v_state = k3, v3
        hx = _post_mixer(y.reshape(n, d_ssm), z, o.reshape(n, -1), x2, p, tm, alpha)
        bucket = hx[:, d + ROUTE_BUCKET].astype(jnp.int32)
        pos, src2d, elo, ehi, nused = _route_tables(bucket, N_EXPERT_GROUPS, MOE_TILE)
        y_sorted = _moe(hx, src2d, elo, ehi, nused, p, d)
        x2 = _finalize(y_sorted, pos.reshape(n // tm_fin, tm_fin), hx, p, d, alpha)
        head_dim = k3.shape[2] // heads
        ks.append(k_state.astype(F32).reshape(b, -1, heads, head_dim))
        vs.append(v_state.astype(F32).reshape(b, -1, heads, head_dim))
        hs.append(jnp.swapaxes(ht, 2, 3).reshape(h0[i].shape))
        full = jnp.concatenate([conv_prev[i], xbc3], axis=1) if t < CONV_W - 1 else xbc3
        cs.append(full[:, full.shape[1] - (CONV_W - 1):])
    return x2.reshape(b, t, d), jnp.stack(ks), jnp.stack(vs), jnp.stack(hs), jnp.stack(cs)


def kernel(x_prompt, x_sample, cache_k, cache_v, state_ssm, state_conv, w_in, conv_w, conv_b, dt_bias, a_log,
           d_skip, ssd_norm_g, attn_norm_g, rel_bias, w_out, ln1_g, ln1_b, router_w, router_bias, w_gate, w_up,
           w_down, ln2_g, ln2_b):
    depth = w_in.shape[0]
    heads = cache_k.shape[3]
    d_att = heads * cache_k.shape[4]
    d_conv = state_conv.shape[3]
    ssm_heads = state_ssm.shape[2]
    d_ssm = ssm_heads * state_ssm.shape[3]
    alpha = (2 * depth) ** 0.25
    dims = (d_ssm, d_conv, ssm_heads, d_att, heads)
    layers = [_layer_params(i, w_in, conv_w, conv_b, dt_bias, a_log, d_skip, ssd_norm_g, attn_norm_g, w_out,
                            ln1_g, ln1_b, router_w, router_bias, w_gate, w_up, w_down, ln2_g, ln2_b, dims)
              for i in range(depth)]
    bp, tp, _ = x_prompt.shape
    bs, ts, _ = x_sample.shape

    conv0 = jnp.zeros((depth, bp, CONV_W - 1, d_conv), F32)
    h00 = jnp.zeros((depth, bp) + state_ssm.shape[2:], F32)
    bias_p = [_band_bias(rel_bias[i], ATT_SUB, BAND_PAST + ATT_SUB, True) for i in range(depth)]
    y_p, k_p, v_p, h_p, c_p = _trunk(x_prompt, layers, bias_p, conv0, h00, None, None, alpha, heads)

    past = cache_k.shape[2]
    bias_s = [_band_bias(rel_bias[i], ts, past + ts, False) for i in range(depth)]
    ck = cache_k.reshape(depth, bs, past, d_att).astype(BF16)
    cv = cache_v.reshape(depth, bs, past, d_att).astype(BF16)
    y_s, k_s, v_s, h_s, c_s = _trunk(x_sample, layers, bias_s, state_conv, state_ssm, ck, cv, alpha, heads)
    return (y_p, y_s, k_p, v_p, h_p, c_p, k_s, v_s, h_s, c_s)
```

```python
import functools

import jax
import jax.numpy as jnp
from jax import lax
from jax.experimental import pallas as pl
from jax.experimental.pallas import tpu as pltpu

F32 = jnp.float32
BF16 = jnp.bfloat16

CHUNK = 64
BAND_PREV_CHUNKS = 8
BAND_PAST = BAND_PREV_CHUNKS * CHUNK
REL_CLIP = 256
SSM_GROUPS = 2
CONV_W = 4
N_EXPERT_GROUPS = 4
EXPERTS_PER_GROUP = 4
PAIRS = [(lo, hi) for lo in range(EXPERTS_PER_GROUP) for hi in range(lo + 1, EXPERTS_PER_GROUP)]
PAIRS_PER_GROUP = len(PAIRS)
ROUTE_BUCKET, ROUTE_GATE_LO, ROUTE_GATE_HI = 0, 1, 2
MOE_TILE = 256
MOE_TILE_SMALL = 64
ATT_SUB = 4 * CHUNK
SSD_SUB = 128
NORM_EPS = 1e-5
NEG_BIG = -1e30

LANES = 128
SUBLANES = 8
VMEM_LIMIT = 56 * 1024 * 1024
ROUTER_ROWS = SUBLANES * EXPERTS_PER_GROUP


def _cparams(sem):
    return pltpu.CompilerParams(dimension_semantics=sem, vmem_limit_bytes=VMEM_LIMIT)


def _silu(v):
    half = 0.5 * v
    return half + half * jnp.tanh(half)


def _split3(v):
    p1 = v.astype(BF16)
    r1 = v - p1.astype(F32)
    p2 = r1.astype(BF16)
    p3 = (r1 - p2.astype(F32)).astype(BF16)
    return p1, p2, p3


def _full(shape):
    return pl.BlockSpec(shape, lambda *_: (0,) * len(shape))


def _in_proj_kernel(x_ref, wz_ref, wxbc_ref, wdt_ref, wq_ref, wk_ref, wv_ref,
                    z_ref, xbc_ref, dt_ref, q_ref, k_ref, v_ref, *, q_scale):
    xb = x_ref[...].astype(BF16)
    z_ref[...] = jnp.dot(xb, wz_ref[...], preferred_element_type=F32)
    xbc_ref[...] = jnp.dot(xb, wxbc_ref[...], preferred_element_type=F32)
    dt_ref[...] = jnp.dot(xb, wdt_ref[...], preferred_element_type=F32)
    q_ref[...] = (jnp.dot(xb, wq_ref[...], preferred_element_type=F32) * q_scale).astype(BF16)
    k_ref[...] = jnp.dot(xb, wk_ref[...], preferred_element_type=F32).astype(BF16)
    v_ref[...] = jnp.dot(xb, wv_ref[...], preferred_element_type=F32).astype(BF16)


def _in_proj(x2d, w, tm):
    n, d = x2d.shape
    d_ssm, d_conv, d_att = w['wz'].shape[1], w['wxbc'].shape[1], w['wq'].shape[1]
    row = lambda width: pl.BlockSpec((tm, width), lambda i: (i, 0))
    return pl.pallas_call(
        functools.partial(_in_proj_kernel, q_scale=w['q_scale']),
        grid=(n // tm,),
        in_specs=[row(d), _full(w['wz'].shape), _full(w['wxbc'].shape), _full(w['wdt'].shape),
                  _full(w['wq'].shape), _full(w['wk'].shape), _full(w['wv'].shape)],
        out_specs=[row(d_ssm), row(d_conv), row(LANES), row(d_att), row(d_att), row(d_att)],
        out_shape=[jax.ShapeDtypeStruct((n, d_ssm), F32), jax.ShapeDtypeStruct((n, d_conv), F32),
                   jax.ShapeDtypeStruct((n, LANES), F32), jax.ShapeDtypeStruct((n, d_att), BF16),
                   jax.ShapeDtypeStruct((n, d_att), BF16), jax.ShapeDtypeStruct((n, d_att), BF16)],
        compiler_params=_cparams(("parallel",)),
        name="in_proj",
    )(x2d, w['wz'], w['wxbc'], w['wdt'], w['wq'], w['wk'], w['wv'])


def _ssd_kernel(xbc_ref, dt_ref, cprev_ref, h0_ref, convw_ref, convb_ref, dtb_ref, alog_ref,
                dskip_ref, tril_ref, ex_ref, y_ref, hout_ref, cbuf, hst, xc_s, dt_s,
                *, blk, sub, d_ssm, d_state, head_dim):
    c = pl.program_id(1)
    gw = d_ssm // SSM_GROUPS

    @pl.when(c == 0)
    def _():
        hst[...] = h0_ref[0]
        cbuf[0:SUBLANES, :] = cprev_ref[0]

    cbuf[SUBLANES:SUBLANES + blk, :] = xbc_ref[0]
    acc = jnp.broadcast_to(convb_ref[...], (blk, convb_ref.shape[1]))
    full = cbuf[...]
    for i in range(CONV_W):
        back = CONV_W - 1 - i
        shifted = full if back == 0 else pltpu.roll(full, back, axis=0)
        acc = acc + shifted[SUBLANES:SUBLANES + blk, :] * convw_ref[i:i + 1, :]
    cbuf[0:SUBLANES, :] = cbuf[blk:blk + SUBLANES, :]
    xc_s[...] = _silu(acc)

    dtr = dt_ref[0] + dtb_ref[...]
    dt_s[...] = jnp.maximum(dtr, 0.0) + jnp.log1p(jnp.exp(-jnp.abs(dtr)))
    a = -jnp.exp(alog_ref[...])

    row = lax.broadcasted_iota(jnp.int32, (sub, sub), 0)
    col = lax.broadcasted_iota(jnp.int32, (sub, sub), 1)
    causal = row >= col
    lane = lax.broadcasted_iota(jnp.int32, (sub, LANES), 1)
    first_head = lane < head_dim
    assert 2 * head_dim == LANES
    pairs_per_group = gw // LANES

    for sc in range(blk // sub):
        r0 = sc * sub
        xs = xc_s[r0:r0 + sub, :d_ssm]
        dt = dt_s[r0:r0 + sub, :]
        acs = sum(jnp.dot(tril_ref[...], piece, preferred_element_type=F32) for piece in _split3(dt * a))
        acs_t = acs.T
        acs_e = sum(jnp.dot(piece, ex_ref[...], preferred_element_type=F32) for piece in _split3(acs))
        dt_e = sum(jnp.dot(piece, ex_ref[...], preferred_element_type=F32) for piece in _split3(dt))
        last_e = acs_e[sub - 1:sub, :]
        dtx = dt_e * xs
        xw = (jnp.exp(last_e - acs_e) * dtx).astype(BF16)
        dtx = dtx.astype(BF16)
        eacs_e = jnp.exp(acs_e)
        cdec_e = jnp.exp(last_e)
        for g in range(SSM_GROUPS):
            b0 = d_ssm + g * d_state
            c0 = d_ssm + (SSM_GROUPS + g) * d_state
            bm = xc_s[r0:r0 + sub, b0:b0 + d_state].astype(BF16)
            cm = xc_s[r0:r0 + sub, c0:c0 + d_state].astype(BF16)
            hg = hst[g]
            cb = lax.dot_general(cm, bm, (((1,), (1,)), ((), ())), preferred_element_type=F32)
            y_off = jnp.dot(cm, hg.astype(BF16), preferred_element_type=F32)
            for jp in range(pairs_per_group):
                p0 = g * gw + jp * LANES
                h0 = p0 // head_dim
                xpair = dtx[:, p0:p0 + LANES]
                outs = []
                for hh in (h0, h0 + 1):
                    seg = acs[:, hh:hh + 1] - acs_t[hh:hh + 1, :]
                    dec = jnp.exp(jnp.where(causal, seg, NEG_BIG))
                    outs.append(jnp.dot((cb * dec).astype(BF16), xpair, preferred_element_type=F32))
                y_diag = jnp.where(first_head, outs[0], outs[1])
                sl = slice(p0, p0 + LANES)
                y_ref[0, r0:r0 + sub, sl] = (y_diag + y_off[:, jp * LANES:(jp + 1) * LANES] * eacs_e[:, sl]
                                             + dskip_ref[:, sl] * xs[:, sl])
            st = lax.dot_general(bm, xw[:, g * gw:(g + 1) * gw], (((0,), (0,)), ((), ())),
                                 preferred_element_type=F32)
            hst[g] = hg * cdec_e[:, g * gw:(g + 1) * gw] + st

    @pl.when(c == pl.num_programs(1) - 1)
    def _():
        hout_ref[0] = hst[...]


def _ssd(xbc, dt, cprev8, h0t, p, blk):
    b, t, d_conv = xbc.shape
    d_ssm = p['dskip_e'].shape[1]
    d_state = (d_conv - d_ssm) // (2 * SSM_GROUPS)
    gw = d_ssm // SSM_GROUPS
    sub = min(blk, SSD_SUB)
    tril = jnp.tril(jnp.ones((sub, sub), BF16))
    kern = functools.partial(_ssd_kernel, blk=blk, sub=sub, d_ssm=d_ssm, d_state=d_state,
                             head_dim=d_ssm // p['a_log_heads'])
    return pl.pallas_call(
        kern,
        grid=(b, t // blk),
        in_specs=[pl.BlockSpec((1, blk, d_conv), lambda i, c: (i, c, 0)),
                  pl.BlockSpec((1, blk, LANES), lambda i, c: (i, c, 0)),
                  pl.BlockSpec((1, SUBLANES, d_conv), lambda i, c: (i, 0, 0)),
                  pl.BlockSpec((1, SSM_GROUPS, d_state, gw), lambda i, c: (i, 0, 0, 0)),
                  _full(p['conv_w'].shape), _full(p['conv_b'].shape), _full(p['dt_bias'].shape),
                  _full(p['a_log'].shape), _full(p['dskip_e'].shape), _full(tril.shape),
                  _full(p['expand'].shape)],
        out_specs=[pl.BlockSpec((1, blk, d_ssm), lambda i, c: (i, c, 0)),
                   pl.BlockSpec((1, SSM_GROUPS, d_state, gw), lambda i, c: (i, 0, 0, 0))],
        out_shape=[jax.ShapeDtypeStruct((b, t, d_ssm), F32),
                   jax.ShapeDtypeStruct((b, SSM_GROUPS, d_state, gw), F32)],
        scratch_shapes=[pltpu.VMEM((blk + SUBLANES, d_conv), F32),
                        pltpu.VMEM((SSM_GROUPS, d_state, gw), F32),
                        pltpu.VMEM((blk, d_conv), F32), pltpu.VMEM((blk, LANES), F32)],
        compiler_params=_cparams(("parallel", "arbitrary")),
        name="ssd",
    )(xbc, dt, cprev8, h0t, p['conv_w'], p['conv_b'], p['dt_bias'], p['a_log'], p['dskip_e'], tril,
      p['expand'])


def _attn_kernel(q_ref, kp_ref, kc_ref, vp_ref, vc_ref, bias_ref, o_ref, kwin, vwin,
                 *, past, qb, sub, kw, heads, head_dim, mask_positions):
    c = pl.program_id(1)
    kwin[0:past, :] = kp_ref[0]
    kwin[past:past + qb, :] = kc_ref[0]
    vwin[0:past, :] = vp_ref[0]
    vwin[past:past + qb, :] = vc_ref[0]
    lane = lax.broadcasted_iota(jnp.int32, (sub, LANES), 1)
    first_head = lane < head_dim

    def sub_block(i, masked):
        r0 = pl.multiple_of(i * sub, sub)
        if masked:
            col = lax.broadcasted_iota(jnp.int32, (2 * sub, kw), 1)
            valid = (col + (c * qb - past + i * sub)) >= 0
        for jp in range(heads * head_dim // LANES):
            ls = slice(jp * LANES, (jp + 1) * LANES)
            q2 = q_ref[0, pl.ds(r0, sub), ls]
            k2 = kwin[pl.ds(r0, kw), ls]
            v2 = vwin[pl.ds(r0, kw), ls]
            zero = jnp.zeros_like(q2)
            qq = jnp.concatenate([jnp.where(first_head, q2, zero), jnp.where(first_head, zero, q2)], axis=0)
            s = lax.dot_general(qq, k2, (((1,), (1,)), ((), ())), preferred_element_type=F32)
            s = s + bias_ref[jp]
            if masked:
                s = jnp.where(valid, s, NEG_BIG)
            m = jnp.max(s, axis=-1, keepdims=True)
            e = jnp.exp(s - m)
            l = jnp.sum(e, axis=-1, keepdims=True)
            o = jnp.dot(e.astype(BF16), v2, preferred_element_type=F32) / l
            o_ref[0, pl.ds(r0, sub), ls] = jnp.where(first_head, o[:sub], o[sub:])

    def run(masked):
        def body(i, carry):
            sub_block(i, masked)
            return carry
        lax.fori_loop(0, qb // sub, body, 0)

    if mask_positions:
        pl.when(c == 0)(lambda: run(True))
        pl.when(c > 0)(lambda: run(False))
    else:
        run(False)


def _attention(q, k, v, k_past, v_past, bias, *, qb, sub, mask_positions, heads):
    b, t, d = q.shape
    kw = bias.shape[2]
    bias = bias.reshape(heads // 2, 2 * sub, kw)
    if k_past is None:
        past = qb
        prev_spec = pl.BlockSpec((1, past, d), lambda i, c: (i, jnp.maximum(c - 1, 0), 0))
        k_past, v_past = k, v
    else:
        assert t == qb
        past = k_past.shape[1]
        prev_spec = pl.BlockSpec((1, past, d), lambda i, c: (i, 0, 0))
    cur_spec = pl.BlockSpec((1, qb, d), lambda i, c: (i, c, 0))
    kern = functools.partial(_attn_kernel, past=past, qb=qb, sub=sub, kw=kw, heads=heads,
                             head_dim=d // heads, mask_positions=mask_positions)
    return pl.pallas_call(
        kern,
        grid=(b, t // qb),
        in_specs=[cur_spec, prev_spec, cur_spec, prev_spec, cur_spec, _full(bias.shape)],
        out_specs=pl.BlockSpec((1, qb, d), lambda i, c: (i, c, 0)),
        out_shape=jax.ShapeDtypeStruct((b, t, d), F32),
        scratch_shapes=[pltpu.VMEM((past + qb, d), BF16), pltpu.VMEM((past + qb, d), BF16)],
        compiler_params=_cparams(("parallel", "arbitrary")),
        name="attention",
    )(q, k_past, k, v_past, v, bias)


def _band_bias(table, sub, kw, chunked):
    assert kw - sub == BAND_PAST or not chunked
    p_len = sub + kw
    dist = (kw - sub) + (sub - 1) - jnp.arange(p_len)
    vec = table[jnp.clip(dist, -REL_CLIP, REL_CLIP) + REL_CLIP].T.astype(F32)
    skew = jnp.tile(vec, (1, sub))[:, :sub * (p_len - 1)].reshape(-1, sub, p_len - 1)
    bias = skew[:, :, sub - 1:sub - 1 + kw]
    if chunked:
        s_loc = jnp.arange(kw)[None, :] - (jnp.arange(sub)[:, None] // CHUNK) * CHUNK
        bias = jnp.where(((s_loc >= 0) & (s_loc < BAND_PAST + CHUNK))[None], bias, NEG_BIG)
    return bias


def _layer_norm(r, g, b):
    mu = jnp.mean(r, axis=-1, keepdims=True)
    d = r - mu
    var = jnp.mean(d * d, axis=-1, keepdims=True)
    return d * lax.rsqrt(var + NORM_EPS) * g + b


def _post_mixer_kernel(y_ref, z_ref, o_ref, x_ref, wo_ref, gs_ref, ga_ref, lg_ref, lb_ref, rwt_ref, rb_ref,
                       hx_ref, cscr, *, alpha, n_experts, d_ssm):
    u = y_ref[...] * _silu(z_ref[...])
    ys = u * lax.rsqrt(jnp.mean(u * u, axis=-1, keepdims=True) + NORM_EPS) * gs_ref[...]
    o = o_ref[...]
    oa = o * lax.rsqrt(jnp.mean(o * o, axis=-1, keepdims=True) + NORM_EPS) * ga_ref[...]
    mix = (jnp.dot(ys.astype(BF16), wo_ref[0:d_ssm, :], preferred_element_type=F32)
           + jnp.dot(oa.astype(BF16), wo_ref[d_ssm:, :], preferred_element_type=F32))
    h = _layer_norm(alpha * x_ref[...] + mix, lg_ref[...], lb_ref[...])

    epg = EXPERTS_PER_GROUP
    assert n_experts == N_EXPERT_GROUPS * epg and epg == 4 and N_EXPERT_GROUPS <= SUBLANES
    h0 = h.astype(BF16)
    h1 = (h - h0.astype(F32)).astype(BF16)
    contract_last = (((1,), (1,)), ((), ()))
    lt0 = lax.dot_general(rwt_ref[...], h0, contract_last, preferred_element_type=F32)
    lt1 = lax.dot_general(rwt_ref[...], h1, contract_last, preferred_element_type=F32)
    rr = ROUTER_ROWS
    logits = lt0[0:rr] + lt0[rr:2 * rr] + lt0[2 * rr:3 * rr] + lt1[0:rr] + lt1[rr:2 * rr]
    grp = lax.broadcasted_iota(jnp.int32, (SUBLANES, logits.shape[1]), 0)
    real = grp < N_EXPERT_GROUPS
    lg = [jnp.where(real, logits[SUBLANES * j:SUBLANES * (j + 1)], -jnp.inf) for j in range(epg)]
    mx = jnp.max(functools.reduce(jnp.maximum, lg), axis=0, keepdims=True)
    ex = [jnp.exp(v - mx) for v in lg]
    zsum = jnp.sum(functools.reduce(lambda p, q: p + q, ex), axis=0, keepdims=True)
    score = [e / zsum for e in ex]
    sel = [score[j] + rb_ref[SUBLANES * j:SUBLANES * (j + 1), :] for j in range(epg)]

    def top2_sum(v):
        a, b = jnp.maximum(v[0], v[1]), jnp.minimum(v[0], v[1])
        c, d = jnp.maximum(v[2], v[3]), jnp.minimum(v[2], v[3])
        return jnp.maximum(a, c) + jnp.maximum(jnp.minimum(a, c), jnp.maximum(b, d))

    gscore = jnp.where(real, top2_sum(sel), -jnp.inf)
    gbest = jnp.max(gscore, axis=0, keepdims=True)
    gidx = jnp.min(jnp.where(gscore == gbest, grp, SUBLANES), axis=0, keepdims=True)
    chosen = grp == gidx
    in_sel = [jnp.sum(jnp.where(chosen, v, 0.0), axis=0, keepdims=True) for v in sel]
    in_score = [jnp.sum(jnp.where(chosen, v, 0.0), axis=0, keepdims=True) for v in score]

    def argmax_first(vals, exclude=None):
        bv, bi = None, None
        for j, v in enumerate(vals):
            if exclude is not None:
                v = jnp.where(exclude == j, -jnp.inf, v)
            if bv is None:
                bv, bi = v, jnp.zeros_like(gidx)
            else:
                upd = v > bv
                bv = jnp.where(upd, v, bv)
                bi = jnp.where(upd, j, bi)
        return bi

    j1 = argmax_first(in_sel)
    j2 = argmax_first(in_sel, exclude=j1)
    s1 = functools.reduce(lambda p, q: p + q, [jnp.where(j1 == j, in_score[j], 0.0) for j in range(epg)])
    s2 = functools.reduce(lambda p, q: p + q, [jnp.where(j2 == j, in_score[j], 0.0) for j in range(epg)])
    g1, g2 = s1 / (s1 + s2), s2 / (s1 + s2)
    first_lo = j1 < j2
    lo, hi = jnp.where(first_lo, j1, j2), jnp.where(first_lo, j2, j1)
    pair = jnp.where(lo == 0, hi - 1, jnp.where(lo == 1, hi + 1, PAIRS_PER_GROUP - 1))
    bucket = gidx * PAIRS_PER_GROUP + pair
    cscr[...] = jnp.zeros_like(cscr)
    cscr[ROUTE_BUCKET:ROUTE_BUCKET + 1, :] = bucket.astype(F32)
    cscr[ROUTE_GATE_LO:ROUTE_GATE_LO + 1, :] = jnp.where(first_lo, g1, g2)
    cscr[ROUTE_GATE_HI:ROUTE_GATE_HI + 1, :] = jnp.where(first_lo, g2, g1)
    d = h.shape[1]
    hx_ref[:, :d] = h
    hx_ref[:, d:] = cscr[...].T


def _post_mixer(y, z, o, x, p, tm, alpha):
    n, d = x.shape
    d_ssm = y.shape[1]
    row = lambda width: pl.BlockSpec((tm, width), lambda i: (i, 0))
    n_experts = p['n_experts']
    kern = functools.partial(_post_mixer_kernel, alpha=alpha, n_experts=n_experts, d_ssm=d_ssm)
    return pl.pallas_call(
        kern,
        grid=(n // tm,),
        in_specs=[row(d_ssm), row(d_ssm), row(o.shape[1]), row(d), _full(p['w_out'].shape),
                  _full(p['ssd_norm_g'].shape), _full(p['attn_norm_g'].shape), _full(p['ln1_g'].shape),
                  _full(p['ln1_b'].shape), _full(p['router_wt'].shape), _full(p['router_bias'].shape)],
        out_specs=row(d + LANES),
        out_shape=jax.ShapeDtypeStruct((n, d + LANES), F32),
        scratch_shapes=[pltpu.VMEM((LANES, tm), F32)],
        compiler_params=_cparams(("parallel",)),
        name="post_mixer",
    )(y, z, o, x, p['w_out'], p['ssd_norm_g'], p['attn_norm_g'], p['ln1_g'], p['ln1_b'], p['router_wt'],
      p['router_bias'])


def _start_row_gather(idx_ref, tile, src_hbm, buf, sem, slot, rows):
    for r in range(rows):
        row = idx_ref[tile, r]
        pltpu.make_async_copy(src_hbm.at[pl.ds(row, 1), :], buf.at[slot, pl.ds(r, 1), :], sem.at[slot]).start()


def _wait_row_gather(src_hbm, buf, sem, slot, rows):
    pltpu.make_async_copy(src_hbm.at[pl.ds(0, rows), :], buf.at[slot], sem.at[slot]).wait()


def _prefetched_rows(idx_ref, src_hbm, buf, sem, rows):
    i = pl.program_id(0)
    slot = lax.rem(i, 2)

    @pl.when(i == 0)
    def _():
        _start_row_gather(idx_ref, 0, src_hbm, buf, sem, 0, rows)

    @pl.when(i + 1 < pl.num_programs(0))
    def _():
        _start_row_gather(idx_ref, i + 1, src_hbm, buf, sem, 1 - slot, rows)

    _wait_row_gather(src_hbm, buf, sem, slot, rows)
    return slot


def _moe_kernel(src_ref, elo_ref, ehi_ref, nused_ref, hx_hbm, wg_lo, wu_lo, wd_lo, wg_hi, wu_hi, wd_hi,
                y_ref, xbuf, sem, *, tm, d):
    del elo_ref, ehi_ref
    slot = _prefetched_rows(src_ref, hx_hbm, xbuf, sem, tm)

    @pl.when(pl.program_id(0) < nused_ref[0])
    def _():
        x = xbuf[slot]
        xb = x[:, :d].astype(BF16)

        def ffn(wg, wu, wd):
            gate = jnp.dot(xb, wg[0], preferred_element_type=F32)
            up = jnp.dot(xb, wu[0], preferred_element_type=F32)
            return jnp.dot((_silu(gate) * up).astype(BF16), wd[0], preferred_element_type=F32)

        y_ref[...] = (x[:, d + ROUTE_GATE_LO:d + ROUTE_GATE_LO + 1] * ffn(wg_lo, wu_lo, wd_lo)
                      + x[:, d + ROUTE_GATE_HI:d + ROUTE_GATE_HI + 1] * ffn(wg_hi, wu_hi, wd_hi))

    @pl.when(pl.program_id(0) >= nused_ref[0])
    def _():
        y_ref[...] = jnp.zeros_like(y_ref)


def _moe(hx, src2d, elo, ehi, nused, p, d):
    n_tiles, tm = src2d.shape
    _, _, d_exp = p['w_gate'].shape
    wspec = lambda sel: pl.BlockSpec((1, d, d_exp), lambda i, s, lo, hi, nu: ((lo if sel == 0 else hi)[i], 0, 0))
    wdspec = lambda sel: pl.BlockSpec((1, d_exp, d), lambda i, s, lo, hi, nu: ((lo if sel == 0 else hi)[i], 0, 0))
    return pl.pallas_call(
        functools.partial(_moe_kernel, tm=tm, d=d),
        grid_spec=pltpu.PrefetchScalarGridSpec(
            num_scalar_prefetch=4,
            grid=(n_tiles,),
            in_specs=[pl.BlockSpec(memory_space=pl.ANY), wspec(0), wspec(0), wdspec(0),
                      wspec(1), wspec(1), wdspec(1)],
            out_specs=pl.BlockSpec((tm, d), lambda i, *_: (i, 0)),
            scratch_shapes=[pltpu.VMEM((2, tm, hx.shape[1]), F32), pltpu.SemaphoreType.DMA((2,))]),
        out_shape=jax.ShapeDtypeStruct((n_tiles * tm, d), F32),
        compiler_params=_cparams(("arbitrary",)),
        name="moe",
    )(src2d, elo, ehi, nused, hx, p['w_gate'], p['w_up'], p['w_down'], p['w_gate'], p['w_up'], p['w_down'])


def _finalize_kernel(pos_ref, y_hbm, hx_ref, lg_ref, lb_ref, out_ref, ybuf, sem, *, tm, d, alpha):
    slot = _prefetched_rows(pos_ref, y_hbm, ybuf, sem, tm)
    out_ref[...] = _layer_norm(alpha * hx_ref[:, :d] + ybuf[slot], lg_ref[...], lb_ref[...])


def _finalize(y_sorted, pos2d, hx, p, d, alpha):
    n_tiles, tm = pos2d.shape
    return pl.pallas_call(
        functools.partial(_finalize_kernel, tm=tm, d=d, alpha=alpha),
        grid_spec=pltpu.PrefetchScalarGridSpec(
            num_scalar_prefetch=1,
            grid=(n_tiles,),
            in_specs=[pl.BlockSpec(memory_space=pl.ANY),
                      pl.BlockSpec((tm, hx.shape[1]), lambda i, pos: (i, 0)),
                      pl.BlockSpec(p['ln2_g'].shape, lambda i, pos: (0, 0)),
                      pl.BlockSpec(p['ln2_b'].shape, lambda i, pos: (0, 0))],
            out_specs=pl.BlockSpec((tm, d), lambda i, pos: (i, 0)),
            scratch_shapes=[pltpu.VMEM((2, tm, d), F32), pltpu.SemaphoreType.DMA((2,))]),
        out_shape=jax.ShapeDtypeStruct((n_tiles * tm, d), F32),
        compiler_params=_cparams(("arbitrary",)),
        name="finalize",
    )(pos2d, y_sorted, hx, p['ln2_g'], p['ln2_b'])


def _route_tables(bucket, n_groups, tm):
    n = bucket.shape[0]
    n_buckets = n_groups * PAIRS_PER_GROUP
    n_tiles = n // tm + n_buckets
    onehot = bucket.reshape(n // tm, tm)[:, :, None] == jnp.arange(n_buckets, dtype=jnp.int32)
    tril = jnp.tril(jnp.ones((tm, tm), BF16))
    within = jnp.einsum('ij,tjb->tib', tril, onehot.astype(BF16), preferred_element_type=F32)
    block_counts = within[:, -1, :]
    before = jnp.cumsum(block_counts, axis=0) - block_counts
    counts = (before[-1] + block_counts[-1]).astype(jnp.int32)
    tiles_per = (counts + tm - 1) // tm
    tile_end = jnp.cumsum(tiles_per)
    start = ((tile_end - tiles_per) * tm).astype(F32)
    pos = jnp.sum(jnp.where(onehot, within - 1.0 + before[:, None, :] + start, 0.0), axis=-1)
    pos = pos.reshape(n).astype(jnp.int32)
    src = jnp.zeros((n_tiles * tm,), jnp.int32).at[pos].set(jnp.arange(n, dtype=jnp.int32), unique_indices=True)
    tile_bucket = jnp.sum(tile_end[None, :] <= jnp.arange(n_tiles, dtype=jnp.int32)[:, None], axis=1)
    tile_bucket = jnp.minimum(tile_bucket, n_buckets - 1).astype(jnp.int32)
    group, pair = tile_bucket // PAIRS_PER_GROUP, tile_bucket % PAIRS_PER_GROUP
    elo = group * EXPERTS_PER_GROUP + jnp.asarray([lo for lo, _ in PAIRS], jnp.int32)[pair]
    ehi = group * EXPERTS_PER_GROUP + jnp.asarray([hi for _, hi in PAIRS], jnp.int32)[pair]
    return pos, src.reshape(n_tiles, tm), elo, ehi, tile_end[-1:].astype(jnp.int32)


def _pick(n, candidates):
    for c in candidates:
        if n % c == 0:
            return c
    raise ValueError(f"no tile size for {n}")


def _router_rows(per_expert):
    a = per_expert.reshape(N_EXPERT_GROUPS, EXPERTS_PER_GROUP, -1).swapaxes(0, 1)
    a = jnp.pad(a, ((0, 0), (0, SUBLANES - N_EXPERT_GROUPS), (0, 0)))
    return a.reshape(ROUTER_ROWS, -1)


def _layer_params(i, w_in, conv_w, conv_b, dt_bias, a_log, d_skip, ssd_norm_g, attn_norm_g, w_out,
                  ln1_g, ln1_b, router_w, router_bias, w_gate, w_up, w_down, ln2_g, ln2_b, dims):
    d_ssm, d_conv, heads, d_att, att_heads = dims
    o = 0
    wi = w_in[i]
    wz = wi[:, o:o + d_ssm]; o += d_ssm
    wxbc = wi[:, o:o + d_conv]; o += d_conv
    wdt = wi[:, o:o + heads]; o += heads
    wq = wi[:, o:o + d_att]; o += d_att
    wk = wi[:, o:o + d_att]; o += d_att
    wv = wi[:, o:o + d_att]
    pad_lanes = lambda v: jnp.pad(v, ((0, 0), (0, LANES - v.shape[1])))
    head_dim = d_ssm // heads
    expand = (jnp.arange(LANES)[:, None] == (jnp.arange(d_ssm)[None, :] // head_dim)).astype(BF16)
    return {
        'wz': wz.astype(BF16), 'wxbc': wxbc.astype(BF16), 'wdt': pad_lanes(wdt).astype(BF16),
        'wq': wq.astype(BF16), 'wk': wk.astype(BF16), 'wv': wv.astype(BF16),
        'q_scale': float(d_att // att_heads) ** -0.5,
        'conv_w': conv_w[i], 'conv_b': conv_b[i][None, :],
        'dt_bias': pad_lanes(dt_bias[i][None, :]), 'a_log': pad_lanes(a_log[i][None, :]),
        'dskip_e': jnp.repeat(d_skip[i], head_dim)[None, :], 'expand': expand, 'a_log_heads': heads,
        'ssd_norm_g': ssd_norm_g[i][None, :], 'attn_norm_g': attn_norm_g[i][None, :],
        'w_out': w_out[i].astype(BF16), 'ln1_g': ln1_g[i][None, :], 'ln1_b': ln1_b[i][None, :],
        'router_wt': jnp.pad(jnp.concatenate([_router_rows(piece) for piece in _split3(router_w.T)]),
                             ((0, LANES - 3 * ROUTER_ROWS), (0, 0))),
        'router_bias': _router_rows(router_bias[:, None]), 'n_experts': router_bias.shape[0],
        'w_gate': w_gate[i].astype(BF16), 'w_up': w_up[i].astype(BF16), 'w_down': w_down[i].astype(BF16),
        'ln2_g': ln2_g[i][None, :], 'ln2_b': ln2_b[i][None, :],
    }


def _trunk(x, layers, biases, conv_prev, h0, k_prev, v_prev, alpha, heads):
    b, t, d = x.shape
    n = b * t
    prompt = k_prev is None
    ks, vs, hs, cs = [], [], [], []
    x2 = x.reshape(n, d)
    tm = _pick(n, (512, 256, 128, 64, 32, 16, 8))
    tm_fin = _pick(n, (512, 256))
    n_buckets = N_EXPERT_GROUPS * PAIRS_PER_GROUP
    moe_tile = MOE_TILE if n >= 2 * n_buckets * MOE_TILE else MOE_TILE_SMALL
    assert n % moe_tile == 0
    for i, p in enumerate(layers):
        z, xbc, dt, q, k, v = _in_proj(x2, p, tm)
        d_conv = xbc.shape[1]
        d_ssm = z.shape[1]
        d_state = (d_conv - d_ssm) // (2 * SSM_GROUPS)
        xbc3 = xbc.reshape(b, t, d_conv)
        cprev8 = jnp.pad(conv_prev[i], ((0, 0), (SUBLANES - (CONV_W - 1), 0), (0, 0)))
        h0t = jnp.swapaxes(h0[i].reshape(b, SSM_GROUPS, d_ssm // SSM_GROUPS, d_state), 2, 3)
        blk = _pick(t, (256, 128, 64, 32, 16, 8))
        y, ht = _ssd(xbc3, dt.reshape(b, t, LANES), cprev8, h0t, p, blk)
        q3, k3, v3 = (a.reshape(b, t, -1) for a in (q, k, v))
        if prompt:
            o = _attention(q3, k3, v3, None, None, biases[i], qb=BAND_PAST, sub=ATT_SUB,
                           mask_positions=True, heads=heads)
            keep = min(BAND_PAST, t)
            k_state, v_state = k3[:, t - keep:], v3[:, t - keep:]
        else:
            o = _attention(q3, k3, v3, k_prev[i], v_prev[i], biases[i], qb=t, sub=t,
                           mask_positions=False, heads=heads)
            k_state, v_state = k3, v3
        hx = _post_mixer(y.reshape(n, d_ssm), z, o.reshape(n, -1), x2, p, tm, alpha)
        bucket = hx[:, d + ROUTE_BUCKET].astype(jnp.int32)
        pos, src2d, elo, ehi, nused = _route_tables(bucket, N_EXPERT_GROUPS, moe_tile)
        y_sorted = _moe(hx, src2d, elo, ehi, nused, p, d)
        x2 = _finalize(y_sorted, pos.reshape(n // tm_fin, tm_fin), hx, p, d, alpha)
        head_dim = k3.shape[2] // heads
        ks.append(k_state.astype(F32).reshape(b, -1, heads, head_dim))
        vs.append(v_state.astype(F32).reshape(b, -1, heads, head_dim))
        hs.append(jnp.swapaxes(ht, 2, 3).reshape(h0[i].shape))
        full = jnp.concatenate([conv_prev[i], xbc3], axis=1) if t < CONV_W - 1 else xbc3
        cs.append(full[:, full.shape[1] - (CONV_W - 1):])
    return x2.reshape(b, t, d), jnp.stack(ks), jnp.stack(vs), jnp.stack(hs), jnp.stack(cs)


def kernel(x_prompt, x_sample, cache_k, cache_v, state_ssm, state_conv, w_in, conv_w, conv_b, dt_bias, a_log,
           d_skip, ssd_norm_g, attn_norm_g, rel_bias, w_out, ln1_g, ln1_b, router_w, router_bias, w_gate, w_up,
           w_down, ln2_g, ln2_b):
    depth = w_in.shape[0]
    heads = cache_k.shape[3]
    d_att = heads * cache_k.shape[4]
    d_conv = state_conv.shape[3]
    ssm_heads = state_ssm.shape[2]
    d_ssm = ssm_heads * state_ssm.shape[3]
    alpha = (2 * depth) ** 0.25
    dims = (d_ssm, d_conv, ssm_heads, d_att, heads)
    layers = [_layer_params(i, w_in, conv_w, conv_b, dt_bias, a_log, d_skip, ssd_norm_g, attn_norm_g, w_out,
                            ln1_g, ln1_b, router_w, router_bias, w_gate, w_up, w_down, ln2_g, ln2_b, dims)
              for i in range(depth)]
    bp, tp, _ = x_prompt.shape
    bs, ts, _ = x_sample.shape

    conv0 = jnp.zeros((depth, bp, CONV_W - 1, d_conv), F32)
    h00 = jnp.zeros((depth, bp) + state_ssm.shape[2:], F32)
    bias_p = [_band_bias(rel_bias[i], ATT_SUB, BAND_PAST + ATT_SUB, True) for i in range(depth)]
    y_p, k_p, v_p, h_p, c_p = _trunk(x_prompt, layers, bias_p, conv0, h00, None, None, alpha, heads)

    past = cache_k.shape[2]
    bias_s = [_band_bias(rel_bias[i], ts, past + ts, False) for i in range(depth)]
    ck = cache_k.reshape(depth, bs, past, d_att).astype(BF16)
    cv = cache_v.reshape(depth, bs, past, d_att).astype(BF16)
    y_s, k_s, v_s, h_s, c_s = _trunk(x_sample, layers, bias_s, state_conv, state_ssm, ck, cv, alpha, heads)
    return (y_p, y_s, k_p, v_p, h_p, c_p, k_s, v_s, h_s, c_s)
```

```python
import functools

import jax
import jax.numpy as jnp
from jax import lax
from jax.experimental import pallas as pl
from jax.experimental.pallas import tpu as pltpu

F32 = jnp.float32
BF16 = jnp.bfloat16

CHUNK = 64
BAND_PREV_CHUNKS = 8
BAND_PAST = BAND_PREV_CHUNKS * CHUNK
REL_CLIP = 256
SSM_GROUPS = 2
CONV_W = 4
N_EXPERT_GROUPS = 4
EXPERTS_PER_GROUP = 4
PAIRS = [(lo, hi) for lo in range(EXPERTS_PER_GROUP) for hi in range(lo + 1, EXPERTS_PER_GROUP)]
PAIRS_PER_GROUP = len(PAIRS)
ROUTE_BUCKET, ROUTE_GATE_LO, ROUTE_GATE_HI = 0, 1, 2
MOE_TILE = 256
MOE_TILE_SMALL = 64
ATT_SUB = 4 * CHUNK
SSD_SUB = 128
NORM_EPS = 1e-5
NEG_BIG = -1e30

LANES = 128
SUBLANES = 8
VMEM_LIMIT = 56 * 1024 * 1024
ROUTER_ROWS = SUBLANES * EXPERTS_PER_GROUP


def _cparams(sem):
    return pltpu.CompilerParams(dimension_semantics=sem, vmem_limit_bytes=VMEM_LIMIT)


def _silu(v):
    half = 0.5 * v
    return half + half * jnp.tanh(half)


def _split3(v):
    p1 = v.astype(BF16)
    r1 = v - p1.astype(F32)
    p2 = r1.astype(BF16)
    p3 = (r1 - p2.astype(F32)).astype(BF16)
    return p1, p2, p3


def _full(shape):
    return pl.BlockSpec(shape, lambda *_: (0,) * len(shape))


def _conv_silu(cbuf, new_rows, out_ref, convw_ref, convb_ref):
    n, width = new_rows.shape
    cbuf[SUBLANES:SUBLANES + n, :] = new_rows
    for c0 in range(0, width, LANES):
        cols = slice(c0, c0 + LANES)
        full = cbuf[:, cols]
        acc = jnp.broadcast_to(convb_ref[:, cols], (n, LANES))
        for i in range(CONV_W):
            back = CONV_W - 1 - i
            shifted = full if back == 0 else pltpu.roll(full, back, axis=0)
            acc = acc + shifted[SUBLANES:SUBLANES + n, :] * convw_ref[i:i + 1, cols]
        out_ref[:, cols] = _silu(acc)
    cbuf[0:SUBLANES, :] = cbuf[n:n + SUBLANES, :]


def _in_proj_kernel(x_ref, wz_ref, wxbc_ref, wdt_ref, wq_ref, wk_ref, wv_ref,
                    z_ref, xbc_ref, dt_ref, q_ref, k_ref, v_ref, *, q_scale):
    xb = x_ref[...].astype(BF16)
    z_ref[...] = jnp.dot(xb, wz_ref[...], preferred_element_type=F32)
    xbc_ref[...] = jnp.dot(xb, wxbc_ref[...], preferred_element_type=F32)
    dt_ref[...] = jnp.dot(xb, wdt_ref[...], preferred_element_type=F32)
    q_ref[...] = (jnp.dot(xb, wq_ref[...], preferred_element_type=F32) * q_scale).astype(BF16)
    k_ref[...] = jnp.dot(xb, wk_ref[...], preferred_element_type=F32).astype(BF16)
    v_ref[...] = jnp.dot(xb, wv_ref[...], preferred_element_type=F32).astype(BF16)


def _in_proj(x2d, w, tm):
    n, d = x2d.shape
    d_ssm, d_conv, d_att = w['wz'].shape[1], w['wxbc'].shape[1], w['wq'].shape[1]
    row = lambda width: pl.BlockSpec((tm, width), lambda i: (i, 0))
    return pl.pallas_call(
        functools.partial(_in_proj_kernel, q_scale=w['q_scale']),
        grid=(n // tm,),
        in_specs=[row(d), _full(w['wz'].shape), _full(w['wxbc'].shape), _full(w['wdt'].shape),
                  _full(w['wq'].shape), _full(w['wk'].shape), _full(w['wv'].shape)],
        out_specs=[row(d_ssm), row(d_conv), row(LANES), row(d_att), row(d_att), row(d_att)],
        out_shape=[jax.ShapeDtypeStruct((n, d_ssm), F32), jax.ShapeDtypeStruct((n, d_conv), F32),
                   jax.ShapeDtypeStruct((n, LANES), F32), jax.ShapeDtypeStruct((n, d_att), BF16),
                   jax.ShapeDtypeStruct((n, d_att), BF16), jax.ShapeDtypeStruct((n, d_att), BF16)],
        compiler_params=_cparams(("parallel",)),
        name="in_proj",
    )(x2d, w['wz'], w['wxbc'], w['wdt'], w['wq'], w['wk'], w['wv'])


def _ssd_kernel(xbc_ref, dt_ref, cprev_ref, h0_ref, convw_ref, convb_ref, dtb_ref, alog_ref,
                dskip_ref, tril_ref, ex_ref, y_ref, hout_ref, cbuf, hst, xc_s, dt_s,
                *, blk, sub, d_ssm, d_state, head_dim):
    c = pl.program_id(1)
    gw = d_ssm // SSM_GROUPS

    @pl.when(c == 0)
    def _():
        hst[...] = h0_ref[0]
        cbuf[0:SUBLANES, :] = cprev_ref[0]

    _conv_silu(cbuf, xbc_ref[0], xc_s, convw_ref, convb_ref)

    dtr = dt_ref[0] + dtb_ref[...]
    dt_s[...] = jnp.maximum(dtr, 0.0) + jnp.log1p(jnp.exp(-jnp.abs(dtr)))
    a = -jnp.exp(alog_ref[...])

    row = lax.broadcasted_iota(jnp.int32, (sub, sub), 0)
    col = lax.broadcasted_iota(jnp.int32, (sub, sub), 1)
    causal = row >= col
    lane = lax.broadcasted_iota(jnp.int32, (sub, LANES), 1)
    first_head = lane < head_dim
    assert 2 * head_dim == LANES
    pairs_per_group = gw // LANES

    for sc in range(blk // sub):
        r0 = sc * sub
        dt = dt_s[r0:r0 + sub, :]
        acs = sum(jnp.dot(tril_ref[...], piece, preferred_element_type=F32) for piece in _split3(dt * a))
        acs_t = acs.T
        acs_p, dt_p = _split3(acs), _split3(dt)
        for g in range(SSM_GROUPS):
            gs = slice(g * gw, (g + 1) * gw)
            xs = xc_s[r0:r0 + sub, gs]
            acs_e = sum(jnp.dot(piece, ex_ref[:, gs], preferred_element_type=F32) for piece in acs_p)
            dt_e = sum(jnp.dot(piece, ex_ref[:, gs], preferred_element_type=F32) for piece in dt_p)
            last_e = acs_e[sub - 1:sub, :]
            dtx = dt_e * xs
            xw = (jnp.exp(last_e - acs_e) * dtx).astype(BF16)
            dtx = dtx.astype(BF16)
            eacs_e = jnp.exp(acs_e)
            b0 = d_ssm + g * d_state
            c0 = d_ssm + (SSM_GROUPS + g) * d_state
            bm = xc_s[r0:r0 + sub, b0:b0 + d_state].astype(BF16)
            cm = xc_s[r0:r0 + sub, c0:c0 + d_state].astype(BF16)
            hg = hst[g]
            cb = lax.dot_general(cm, bm, (((1,), (1,)), ((), ())), preferred_element_type=F32)
            y_off = jnp.dot(cm, hg.astype(BF16), preferred_element_type=F32)
            for jp in range(pairs_per_group):
                lg = slice(jp * LANES, (jp + 1) * LANES)
                p0 = g * gw + jp * LANES
                h0 = p0 // head_dim
                xpair = dtx[:, lg]
                outs = []
                for hh in (h0, h0 + 1):
                    seg = acs[:, hh:hh + 1] - acs_t[hh:hh + 1, :]
                    dec = jnp.exp(jnp.where(causal, seg, NEG_BIG))
                    outs.append(jnp.dot((cb * dec).astype(BF16), xpair, preferred_element_type=F32))
                y_diag = jnp.where(first_head, outs[0], outs[1])
                sl = slice(p0, p0 + LANES)
                y_ref[0, r0:r0 + sub, sl] = y_diag + y_off[:, lg] * eacs_e[:, lg] + dskip_ref[:, sl] * xs[:, lg]
            st = lax.dot_general(bm, xw, (((0,), (0,)), ((), ())), preferred_element_type=F32)
            hst[g] = hg * jnp.exp(last_e) + st

    @pl.when(c == pl.num_programs(1) - 1)
    def _():
        hout_ref[0] = hst[...]


def _ssd(xbc, dt, cprev8, h0t, p, blk):
    b, t, d_conv = xbc.shape
    d_ssm = p['dskip_e'].shape[1]
    d_state = (d_conv - d_ssm) // (2 * SSM_GROUPS)
    gw = d_ssm // SSM_GROUPS
    sub = min(blk, SSD_SUB)
    tril = jnp.tril(jnp.ones((sub, sub), BF16))
    kern = functools.partial(_ssd_kernel, blk=blk, sub=sub, d_ssm=d_ssm, d_state=d_state,
                             head_dim=d_ssm // p['a_log_heads'])
    return pl.pallas_call(
        kern,
        grid=(b, t // blk),
        in_specs=[pl.BlockSpec((1, blk, d_conv), lambda i, c: (i, c, 0)),
                  pl.BlockSpec((1, blk, LANES), lambda i, c: (i, c, 0)),
                  pl.BlockSpec((1, SUBLANES, d_conv), lambda i, c: (i, 0, 0)),
                  pl.BlockSpec((1, SSM_GROUPS, d_state, gw), lambda i, c: (i, 0, 0, 0)),
                  _full(p['conv_w'].shape), _full(p['conv_b'].shape), _full(p['dt_bias'].shape),
                  _full(p['a_log'].shape), _full(p['dskip_e'].shape), _full(tril.shape),
                  _full(p['expand'].shape)],
        out_specs=[pl.BlockSpec((1, blk, d_ssm), lambda i, c: (i, c, 0)),
                   pl.BlockSpec((1, SSM_GROUPS, d_state, gw), lambda i, c: (i, 0, 0, 0))],
        out_shape=[jax.ShapeDtypeStruct((b, t, d_ssm), F32),
                   jax.ShapeDtypeStruct((b, SSM_GROUPS, d_state, gw), F32)],
        scratch_shapes=[pltpu.VMEM((blk + SUBLANES, d_conv), F32),
                        pltpu.VMEM((SSM_GROUPS, d_state, gw), F32),
                        pltpu.VMEM((blk, d_conv), F32), pltpu.VMEM((blk, LANES), F32)],
        compiler_params=_cparams(("parallel", "arbitrary")),
        name="ssd",
    )(xbc, dt, cprev8, h0t, p['conv_w'], p['conv_b'], p['dt_bias'], p['a_log'], p['dskip_e'], tril,
      p['expand'])


def _attn_kernel(q_ref, kp_ref, kc_ref, vp_ref, vc_ref, bias_ref, o_ref, kwin, vwin,
                 *, past, qb, sub, kw, heads, head_dim, mask_positions):
    c = pl.program_id(1)
    kwin[0:past, :] = kp_ref[0]
    kwin[past:past + qb, :] = kc_ref[0]
    vwin[0:past, :] = vp_ref[0]
    vwin[past:past + qb, :] = vc_ref[0]
    lane = lax.broadcasted_iota(jnp.int32, (sub, LANES), 1)
    first_head = lane < head_dim

    def sub_block(i, masked):
        r0 = pl.multiple_of(i * sub, sub)
        if masked:
            col = lax.broadcasted_iota(jnp.int32, (2 * sub, kw), 1)
            valid = (col + (c * qb - past + i * sub)) >= 0
        for jp in range(heads * head_dim // LANES):
            ls = slice(jp * LANES, (jp + 1) * LANES)
            q2 = q_ref[0, pl.ds(r0, sub), ls]
            k2 = kwin[pl.ds(r0, kw), ls]
            v2 = vwin[pl.ds(r0, kw), ls]
            zero = jnp.zeros_like(q2)
            qq = jnp.concatenate([jnp.where(first_head, q2, zero), jnp.where(first_head, zero, q2)], axis=0)
            s = lax.dot_general(qq, k2, (((1,), (1,)), ((), ())), preferred_element_type=F32)
            s = s + bias_ref[jp]
            if masked:
                s = jnp.where(valid, s, NEG_BIG)
            m = jnp.max(s, axis=-1, keepdims=True)
            e = jnp.exp(s - m)
            l = jnp.sum(e, axis=-1, keepdims=True)
            o = jnp.dot(e.astype(BF16), v2, preferred_element_type=F32) / l
            o_ref[0, pl.ds(r0, sub), ls] = jnp.where(first_head, o[:sub], o[sub:])

    def run(masked):
        def body(i, carry):
            sub_block(i, masked)
            return carry
        lax.fori_loop(0, qb // sub, body, 0)

    if mask_positions:
        pl.when(c == 0)(lambda: run(True))
        pl.when(c > 0)(lambda: run(False))
    else:
        run(False)


def _attention(q, k, v, k_past, v_past, bias, *, qb, sub, mask_positions, heads):
    b, t, d = q.shape
    kw = bias.shape[2]
    bias = bias.reshape(heads // 2, 2 * sub, kw)
    if k_past is None:
        past = qb
        prev_spec = pl.BlockSpec((1, past, d), lambda i, c: (i, jnp.maximum(c - 1, 0), 0))
        k_past, v_past = k, v
    else:
        assert t == qb
        past = k_past.shape[1]
        prev_spec = pl.BlockSpec((1, past, d), lambda i, c: (i, 0, 0))
    cur_spec = pl.BlockSpec((1, qb, d), lambda i, c: (i, c, 0))
    kern = functools.partial(_attn_kernel, past=past, qb=qb, sub=sub, kw=kw, heads=heads,
                             head_dim=d // heads, mask_positions=mask_positions)
    return pl.pallas_call(
        kern,
        grid=(b, t // qb),
        in_specs=[cur_spec, prev_spec, cur_spec, prev_spec, cur_spec, _full(bias.shape)],
        out_specs=pl.BlockSpec((1, qb, d), lambda i, c: (i, c, 0)),
        out_shape=jax.ShapeDtypeStruct((b, t, d), F32),
        scratch_shapes=[pltpu.VMEM((past + qb, d), BF16), pltpu.VMEM((past + qb, d), BF16)],
        compiler_params=_cparams(("parallel", "arbitrary")),
        name="attention",
    )(q, k_past, k, v_past, v, bias)


def _band_bias(table, sub, kw, chunked):
    assert kw - sub == BAND_PAST or not chunked
    p_len = sub + kw
    dist = (kw - sub) + (sub - 1) - jnp.arange(p_len)
    vec = table[jnp.clip(dist, -REL_CLIP, REL_CLIP) + REL_CLIP].T.astype(F32)
    skew = jnp.tile(vec, (1, sub))[:, :sub * (p_len - 1)].reshape(-1, sub, p_len - 1)
    bias = skew[:, :, sub - 1:sub - 1 + kw]
    if chunked:
        s_loc = jnp.arange(kw)[None, :] - (jnp.arange(sub)[:, None] // CHUNK) * CHUNK
        bias = jnp.where(((s_loc >= 0) & (s_loc < BAND_PAST + CHUNK))[None], bias, NEG_BIG)
    return bias


def _layer_norm(r, g, b):
    mu = jnp.mean(r, axis=-1, keepdims=True)
    d = r - mu
    var = jnp.mean(d * d, axis=-1, keepdims=True)
    return d * lax.rsqrt(var + NORM_EPS) * g + b


def _post_mixer_kernel(y_ref, z_ref, o_ref, x_ref, wo_ref, gs_ref, ga_ref, lg_ref, lb_ref, rwt_ref, rb_ref,
                       hx_ref, cscr, *, alpha, n_experts, d_ssm):
    u = y_ref[...] * _silu(z_ref[...])
    ys = u * lax.rsqrt(jnp.mean(u * u, axis=-1, keepdims=True) + NORM_EPS) * gs_ref[...]
    o = o_ref[...]
    oa = o * lax.rsqrt(jnp.mean(o * o, axis=-1, keepdims=True) + NORM_EPS) * ga_ref[...]
    mix = (jnp.dot(ys.astype(BF16), wo_ref[0:d_ssm, :], preferred_element_type=F32)
           + jnp.dot(oa.astype(BF16), wo_ref[d_ssm:, :], preferred_element_type=F32))
    h = _layer_norm(alpha * x_ref[...] + mix, lg_ref[...], lb_ref[...])

    epg = EXPERTS_PER_GROUP
    assert n_experts == N_EXPERT_GROUPS * epg and epg == 4 and N_EXPERT_GROUPS <= SUBLANES
    h0 = h.astype(BF16)
    h1 = (h - h0.astype(F32)).astype(BF16)
    contract_last = (((1,), (1,)), ((), ()))
    lt0 = lax.dot_general(rwt_ref[...], h0, contract_last, preferred_element_type=F32)
    lt1 = lax.dot_general(rwt_ref[...], h1, contract_last, preferred_element_type=F32)
    rr = ROUTER_ROWS
    logits = lt0[0:rr] + lt0[rr:2 * rr] + lt0[2 * rr:3 * rr] + lt1[0:rr] + lt1[rr:2 * rr]
    grp = lax.broadcasted_iota(jnp.int32, (SUBLANES, logits.shape[1]), 0)
    real = grp < N_EXPERT_GROUPS
    lg = [jnp.where(real, logits[SUBLANES * j:SUBLANES * (j + 1)], -jnp.inf) for j in range(epg)]
    mx = jnp.max(functools.reduce(jnp.maximum, lg), axis=0, keepdims=True)
    ex = [jnp.exp(v - mx) for v in lg]
    zsum = jnp.sum(functools.reduce(lambda p, q: p + q, ex), axis=0, keepdims=True)
    score = [e / zsum for e in ex]
    sel = [score[j] + rb_ref[SUBLANES * j:SUBLANES * (j + 1), :] for j in range(epg)]

    def top2_sum(v):
        a, b = jnp.maximum(v[0], v[1]), jnp.minimum(v[0], v[1])
        c, d = jnp.maximum(v[2], v[3]), jnp.minimum(v[2], v[3])
        return jnp.maximum(a, c) + jnp.maximum(jnp.minimum(a, c), jnp.maximum(b, d))

    gscore = jnp.where(real, top2_sum(sel), -jnp.inf)
    gbest = jnp.max(gscore, axis=0, keepdims=True)
    gidx = jnp.min(jnp.where(gscore == gbest, grp, SUBLANES), axis=0, keepdims=True)
    chosen = grp == gidx
    in_sel = [jnp.sum(jnp.where(chosen, v, 0.0), axis=0, keepdims=True) for v in sel]
    in_score = [jnp.sum(jnp.where(chosen, v, 0.0), axis=0, keepdims=True) for v in score]

    def argmax_first(vals, exclude=None):
        bv, bi = None, None
        for j, v in enumerate(vals):
            if exclude is not None:
                v = jnp.where(exclude == j, -jnp.inf, v)
            if bv is None:
                bv, bi = v, jnp.zeros_like(gidx)
            else:
                upd = v > bv
                bv = jnp.where(upd, v, bv)
                bi = jnp.where(upd, j, bi)
        return bi

    j1 = argmax_first(in_sel)
    j2 = argmax_first(in_sel, exclude=j1)
    s1 = functools.reduce(lambda p, q: p + q, [jnp.where(j1 == j, in_score[j], 0.0) for j in range(epg)])
    s2 = functools.reduce(lambda p, q: p + q, [jnp.where(j2 == j, in_score[j], 0.0) for j in range(epg)])
    g1, g2 = s1 / (s1 + s2), s2 / (s1 + s2)
    first_lo = j1 < j2
    lo, hi = jnp.where(first_lo, j1, j2), jnp.where(first_lo, j2, j1)
    pair = jnp.where(lo == 0, hi - 1, jnp.where(lo == 1, hi + 1, PAIRS_PER_GROUP - 1))
    bucket = gidx * PAIRS_PER_GROUP + pair
    cscr[...] = jnp.zeros_like(cscr)
    cscr[ROUTE_BUCKET:ROUTE_BUCKET + 1, :] = bucket.astype(F32)
    cscr[ROUTE_GATE_LO:ROUTE_GATE_LO + 1, :] = jnp.where(first_lo, g1, g2)
    cscr[ROUTE_GATE_HI:ROUTE_GATE_HI + 1, :] = jnp.where(first_lo, g2, g1)
    d = h.shape[1]
    hx_ref[:, :d] = h
    hx_ref[:, d:] = cscr[...].T


def _post_mixer(y, z, o, x, p, tm, alpha):
    n, d = x.shape
    d_ssm = y.shape[1]
    row = lambda width: pl.BlockSpec((tm, width), lambda i: (i, 0))
    n_experts = p['n_experts']
    kern = functools.partial(_post_mixer_kernel, alpha=alpha, n_experts=n_experts, d_ssm=d_ssm)
    return pl.pallas_call(
        kern,
        grid=(n // tm,),
        in_specs=[row(d_ssm), row(d_ssm), row(o.shape[1]), row(d), _full(p['w_out'].shape),
                  _full(p['ssd_norm_g'].shape), _full(p['attn_norm_g'].shape), _full(p['ln1_g'].shape),
                  _full(p['ln1_b'].shape), _full(p['router_wt'].shape), _full(p['router_bias'].shape)],
        out_specs=row(d + LANES),
        out_shape=jax.ShapeDtypeStruct((n, d + LANES), F32),
        scratch_shapes=[pltpu.VMEM((LANES, tm), F32)],
        compiler_params=_cparams(("parallel",)),
        name="post_mixer",
    )(y, z, o, x, p['w_out'], p['ssd_norm_g'], p['attn_norm_g'], p['ln1_g'], p['ln1_b'], p['router_wt'],
      p['router_bias'])


def _start_row_gather(idx_ref, tile, src_hbm, buf, sem, slot, rows):
    for r in range(rows):
        row = idx_ref[tile, r]
        pltpu.make_async_copy(src_hbm.at[pl.ds(row, 1), :], buf.at[slot, pl.ds(r, 1), :], sem.at[slot]).start()


def _wait_row_gather(src_hbm, buf, sem, slot, rows):
    pltpu.make_async_copy(src_hbm.at[pl.ds(0, rows), :], buf.at[slot], sem.at[slot]).wait()


def _prefetched_rows(idx_ref, src_hbm, buf, sem, rows):
    i = pl.program_id(0)
    slot = lax.rem(i, 2)

    @pl.when(i == 0)
    def _():
        _start_row_gather(idx_ref, 0, src_hbm, buf, sem, 0, rows)

    @pl.when(i + 1 < pl.num_programs(0))
    def _():
        _start_row_gather(idx_ref, i + 1, src_hbm, buf, sem, 1 - slot, rows)

    _wait_row_gather(src_hbm, buf, sem, slot, rows)
    return slot


def _moe_kernel(src_ref, elo_ref, ehi_ref, nused_ref, hx_hbm, wg_lo, wu_lo, wd_lo, wg_hi, wu_hi, wd_hi,
                y_ref, xbuf, sem, *, tm, d):
    del elo_ref, ehi_ref
    slot = _prefetched_rows(src_ref, hx_hbm, xbuf, sem, tm)

    @pl.when(pl.program_id(0) < nused_ref[0])
    def _():
        x = xbuf[slot]
        xb = x[:, :d].astype(BF16)

        def ffn(wg, wu, wd):
            gate = jnp.dot(xb, wg[0], preferred_element_type=F32)
            up = jnp.dot(xb, wu[0], preferred_element_type=F32)
            return jnp.dot((_silu(gate) * up).astype(BF16), wd[0], preferred_element_type=F32)

        y_ref[...] = (x[:, d + ROUTE_GATE_LO:d + ROUTE_GATE_LO + 1] * ffn(wg_lo, wu_lo, wd_lo)
                      + x[:, d + ROUTE_GATE_HI:d + ROUTE_GATE_HI + 1] * ffn(wg_hi, wu_hi, wd_hi))

    @pl.when(pl.program_id(0) >= nused_ref[0])
    def _():
        y_ref[...] = jnp.zeros_like(y_ref)


def _moe(hx, src2d, elo, ehi, nused, p, d):
    n_tiles, tm = src2d.shape
    _, _, d_exp = p['w_gate'].shape
    wspec = lambda sel: pl.BlockSpec((1, d, d_exp), lambda i, s, lo, hi, nu: ((lo if sel == 0 else hi)[i], 0, 0))
    wdspec = lambda sel: pl.BlockSpec((1, d_exp, d), lambda i, s, lo, hi, nu: ((lo if sel == 0 else hi)[i], 0, 0))
    return pl.pallas_call(
        functools.partial(_moe_kernel, tm=tm, d=d),
        grid_spec=pltpu.PrefetchScalarGridSpec(
            num_scalar_prefetch=4,
            grid=(n_tiles,),
            in_specs=[pl.BlockSpec(memory_space=pl.ANY), wspec(0), wspec(0), wdspec(0),
                      wspec(1), wspec(1), wdspec(1)],
            out_specs=pl.BlockSpec((tm, d), lambda i, *_: (i, 0)),
            scratch_shapes=[pltpu.VMEM((2, tm, hx.shape[1]), F32), pltpu.SemaphoreType.DMA((2,))]),
        out_shape=jax.ShapeDtypeStruct((n_tiles * tm, d), F32),
        compiler_params=_cparams(("arbitrary",)),
        name="moe",
    )(src2d, elo, ehi, nused, hx, p['w_gate'], p['w_up'], p['w_down'], p['w_gate'], p['w_up'], p['w_down'])


def _finalize_kernel(pos_ref, y_hbm, hx_ref, lg_ref, lb_ref, out_ref, ybuf, sem, *, tm, d, alpha):
    slot = _prefetched_rows(pos_ref, y_hbm, ybuf, sem, tm)
    out_ref[...] = _layer_norm(alpha * hx_ref[:, :d] + ybuf[slot], lg_ref[...], lb_ref[...])


def _finalize(y_sorted, pos2d, hx, p, d, alpha):
    n_tiles, tm = pos2d.shape
    return pl.pallas_call(
        functools.partial(_finalize_kernel, tm=tm, d=d, alpha=alpha),
        grid_spec=pltpu.PrefetchScalarGridSpec(
            num_scalar_prefetch=1,
            grid=(n_tiles,),
            in_specs=[pl.BlockSpec(memory_space=pl.ANY),
                      pl.BlockSpec((tm, hx.shape[1]), lambda i, pos: (i, 0)),
                      pl.BlockSpec(p['ln2_g'].shape, lambda i, pos: (0, 0)),
                      pl.BlockSpec(p['ln2_b'].shape, lambda i, pos: (0, 0))],
            out_specs=pl.BlockSpec((tm, d), lambda i, pos: (i, 0)),
            scratch_shapes=[pltpu.VMEM((2, tm, d), F32), pltpu.SemaphoreType.DMA((2,))]),
        out_shape=jax.ShapeDtypeStruct((n_tiles * tm, d), F32),
        compiler_params=_cparams(("arbitrary",)),
        name="finalize",
    )(pos2d, y_sorted, hx, p['ln2_g'], p['ln2_b'])


def _route_tables(bucket, n_groups, tm):
    n = bucket.shape[0]
    n_buckets = n_groups * PAIRS_PER_GROUP
    n_tiles = n // tm + n_buckets
    onehot = bucket.reshape(n // tm, tm)[:, :, None] == jnp.arange(n_buckets, dtype=jnp.int32)
    tril = jnp.tril(jnp.ones((tm, tm), BF16))
    within = jnp.einsum('ij,tjb->tib', tril, onehot.astype(BF16), preferred_element_type=F32)
    block_counts = within[:, -1, :]
    before = jnp.cumsum(block_counts, axis=0) - block_counts
    counts = (before[-1] + block_counts[-1]).astype(jnp.int32)
    tiles_per = (counts + tm - 1) // tm
    tile_end = jnp.cumsum(tiles_per)
    start = ((tile_end - tiles_per) * tm).astype(F32)
    pos = jnp.sum(jnp.where(onehot, within - 1.0 + before[:, None, :] + start, 0.0), axis=-1)
    pos = pos.reshape(n).astype(jnp.int32)
    src = jnp.zeros((n_tiles * tm,), jnp.int32).at[pos].set(jnp.arange(n, dtype=jnp.int32), unique_indices=True)
    tile_bucket = jnp.sum(tile_end[None, :] <= jnp.arange(n_tiles, dtype=jnp.int32)[:, None], axis=1)
    tile_bucket = jnp.minimum(tile_bucket, n_buckets - 1).astype(jnp.int32)
    group, pair = tile_bucket // PAIRS_PER_GROUP, tile_bucket % PAIRS_PER_GROUP
    elo = group * EXPERTS_PER_GROUP + jnp.asarray([lo for lo, _ in PAIRS], jnp.int32)[pair]
    ehi = group * EXPERTS_PER_GROUP + jnp.asarray([hi for _, hi in PAIRS], jnp.int32)[pair]
    return pos, src.reshape(n_tiles, tm), elo, ehi, tile_end[-1:].astype(jnp.int32)


def _pick(n, candidates):
    for c in candidates:
        if n % c == 0:
            return c
    raise ValueError(f"no tile size for {n}")


def _router_rows(per_expert):
    a = per_expert.reshape(N_EXPERT_GROUPS, EXPERTS_PER_GROUP, -1).swapaxes(0, 1)
    a = jnp.pad(a, ((0, 0), (0, SUBLANES - N_EXPERT_GROUPS), (0, 0)))
    return a.reshape(ROUTER_ROWS, -1)


def _layer_params(i, w_in, conv_w, conv_b, dt_bias, a_log, d_skip, ssd_norm_g, attn_norm_g, w_out,
                  ln1_g, ln1_b, router_w, router_bias, w_gate, w_up, w_down, ln2_g, ln2_b, dims):
    d_ssm, d_conv, heads, d_att, att_heads = dims
    o = 0
    wi = w_in[i]
    wz = wi[:, o:o + d_ssm]; o += d_ssm
    wxbc = wi[:, o:o + d_conv]; o += d_conv
    wdt = wi[:, o:o + heads]; o += heads
    wq = wi[:, o:o + d_att]; o += d_att
    wk = wi[:, o:o + d_att]; o += d_att
    wv = wi[:, o:o + d_att]
    pad_lanes = lambda v: jnp.pad(v, ((0, 0), (0, LANES - v.shape[1])))
    head_dim = d_ssm // heads
    expand = (jnp.arange(LANES)[:, None] == (jnp.arange(d_ssm)[None, :] // head_dim)).astype(BF16)
    return {
        'wz': wz.astype(BF16), 'wxbc': wxbc.astype(BF16), 'wdt': pad_lanes(wdt).astype(BF16),
        'wq': wq.astype(BF16), 'wk': wk.astype(BF16), 'wv': wv.astype(BF16),
        'q_scale': float(d_att // att_heads) ** -0.5,
        'conv_w': conv_w[i], 'conv_b': conv_b[i][None, :],
        'dt_bias': pad_lanes(dt_bias[i][None, :]), 'a_log': pad_lanes(a_log[i][None, :]),
        'dskip_e': jnp.repeat(d_skip[i], head_dim)[None, :], 'expand': expand, 'a_log_heads': heads,
        'ssd_norm_g': ssd_norm_g[i][None, :], 'attn_norm_g': attn_norm_g[i][None, :],
        'w_out': w_out[i].astype(BF16), 'ln1_g': ln1_g[i][None, :], 'ln1_b': ln1_b[i][None, :],
        'router_wt': jnp.pad(jnp.concatenate([_router_rows(piece) for piece in _split3(router_w.T)]),
                             ((0, LANES - 3 * ROUTER_ROWS), (0, 0))),
        'router_bias': _router_rows(router_bias[:, None]), 'n_experts': router_bias.shape[0],
        'w_gate': w_gate[i].astype(BF16), 'w_up': w_up[i].astype(BF16), 'w_down': w_down[i].astype(BF16),
        'ln2_g': ln2_g[i][None, :], 'ln2_b': ln2_b[i][None, :],
    }


def _trunk(x, layers, biases, conv_prev, h0, k_prev, v_prev, alpha, heads):
    b, t, d = x.shape
    n = b * t
    prompt = k_prev is None
    ks, vs, hs, cs = [], [], [], []
    x2 = x.reshape(n, d)
    tm = _pick(n, (512, 256, 128, 64, 32, 16, 8))
    tm_fin = _pick(n, (512, 256))
    n_buckets = N_EXPERT_GROUPS * PAIRS_PER_GROUP
    moe_tile = MOE_TILE if n >= 2 * n_buckets * MOE_TILE else MOE_TILE_SMALL
    assert n % moe_tile == 0
    for i, p in enumerate(layers):
        z, xbc, dt, q, k, v = _in_proj(x2, p, tm)
        d_conv = xbc.shape[1]
        d_ssm = z.shape[1]
        d_state = (d_conv - d_ssm) // (2 * SSM_GROUPS)
        xbc3 = xbc.reshape(b, t, d_conv)
        cprev8 = jnp.pad(conv_prev[i], ((0, 0), (SUBLANES - (CONV_W - 1), 0), (0, 0)))
        h0t = jnp.swapaxes(h0[i].reshape(b, SSM_GROUPS, d_ssm // SSM_GROUPS, d_state), 2, 3)
        blk = _pick(t, (512, 256, 128, 64, 32, 16, 8))
        y, ht = _ssd(xbc3, dt.reshape(b, t, LANES), cprev8, h0t, p, blk)
        q3, k3, v3 = (a.reshape(b, t, -1) for a in (q, k, v))
        if prompt:
            o = _attention(q3, k3, v3, None, None, biases[i], qb=BAND_PAST, sub=ATT_SUB,
                           mask_positions=True, heads=heads)
            keep = min(BAND_PAST, t)
            k_state, v_state = k3[:, t - keep:], v3[:, t - keep:]
        else:
            o = _attention(q3, k3, v3, k_prev[i], v_prev[i], biases[i], qb=t, sub=t,
                           mask_positions=False, heads=heads)
            k_state, v_state = k3, v3
        hx = _post_mixer(y.reshape(n, d_ssm), z, o.reshape(n, -1), x2, p, tm, alpha)
        bucket = hx[:, d + ROUTE_BUCKET].astype(jnp.int32)
        pos, src2d, elo, ehi, nused = _route_tables(bucket, N_EXPERT_GROUPS, moe_tile)
        y_sorted = _moe(hx, src2d, elo, ehi, nused, p, d)
        x2 = _finalize(y_sorted, pos.reshape(n // tm_fin, tm_fin), hx, p, d, alpha)
        head_dim = k3.shape[2] // heads
        ks.append(k_state.astype(F32).reshape(b, -1, heads, head_dim))
        vs.append(v_state.astype(F32).reshape(b, -1, heads, head_dim))
        hs.append(jnp.swapaxes(ht, 2, 3).reshape(h0[i].shape))
        full = jnp.concatenate([conv_prev[i], xbc3], axis=1) if t < CONV_W - 1 else xbc3
        cs.append(full[:, full.shape[1] - (CONV_W - 1):])
    return x2.reshape(b, t, d), jnp.stack(ks), jnp.stack(vs), jnp.stack(hs), jnp.stack(cs)


def kernel(x_prompt, x_sample, cache_k, cache_v, state_ssm, state_conv, w_in, conv_w, conv_b, dt_bias, a_log,
           d_skip, ssd_norm_g, attn_norm_g, rel_bias, w_out, ln1_g, ln1_b, router_w, router_bias, w_gate, w_up,
           w_down, ln2_g, ln2_b):
    depth = w_in.shape[0]
    heads = cache_k.shape[3]
    d_att = heads * cache_k.shape[4]
    d_conv = state_conv.shape[3]
    ssm_heads = state_ssm.shape[2]
    d_ssm = ssm_heads * state_ssm.shape[3]
    alpha = (2 * depth) ** 0.25
    dims = (d_ssm, d_conv, ssm_heads, d_att, heads)
    layers = [_layer_params(i, w_in, conv_w, conv_b, dt_bias, a_log, d_skip, ssd_norm_g, attn_norm_g, w_out,
                            ln1_g, ln1_b, router_w, router_bias, w_gate, w_up, w_down, ln2_g, ln2_b, dims)
              for i in range(depth)]
    bp, tp, _ = x_prompt.shape
    bs, ts, _ = x_sample.shape

    conv0 = jnp.zeros((depth, bp, CONV_W - 1, d_conv), F32)
    h00 = jnp.zeros((depth, bp) + state_ssm.shape[2:], F32)
    bias_p = [_band_bias(rel_bias[i], ATT_SUB, BAND_PAST + ATT_SUB, True) for i in range(depth)]
    y_p, k_p, v_p, h_p, c_p = _trunk(x_prompt, layers, bias_p, conv0, h00, None, None, alpha, heads)

    past = cache_k.shape[2]
    bias_s = [_band_bias(rel_bias[i], ts, past + ts, False) for i in range(depth)]
    ck = cache_k.reshape(depth, bs, past, d_att).astype(BF16)
    cv = cache_v.reshape(depth, bs, past, d_att).astype(BF16)
    y_s, k_s, v_s, h_s, c_s = _trunk(x_sample, layers, bias_s, state_conv, state_ssm, ck, cv, alpha, heads)
    return (y_p, y_s, k_p, v_p, h_p, c_p, k_s, v_s, h_s, c_s)
```

```python
import functools

import jax
import jax.numpy as jnp
from jax import lax
from jax.experimental import pallas as pl
from jax.experimental.pallas import tpu as pltpu

F32 = jnp.float32
BF16 = jnp.bfloat16

CHUNK = 64
BAND_PREV_CHUNKS = 8
BAND_PAST = BAND_PREV_CHUNKS * CHUNK
REL_CLIP = 256
SSM_GROUPS = 2
CONV_W = 4
N_EXPERT_GROUPS = 4
EXPERTS_PER_GROUP = 4
PAIRS = [(lo, hi) for lo in range(EXPERTS_PER_GROUP) for hi in range(lo + 1, EXPERTS_PER_GROUP)]
PAIRS_PER_GROUP = len(PAIRS)
ROUTE_BUCKET, ROUTE_GATE_LO, ROUTE_GATE_HI = 0, 1, 2
MOE_TILE = 256
MOE_TILE_SMALL = 64
ATT_SUB = 4 * CHUNK
SSD_SUB = 128
NORM_EPS = 1e-5
NEG_BIG = -1e30

LANES = 128
SUBLANES = 8
VMEM_LIMIT = 56 * 1024 * 1024
ROUTER_ROWS = SUBLANES * EXPERTS_PER_GROUP


def _cparams(sem):
    return pltpu.CompilerParams(dimension_semantics=sem, vmem_limit_bytes=VMEM_LIMIT)


def _silu(v):
    half = 0.5 * v
    return half + half * jnp.tanh(half)


def _split3(v):
    p1 = v.astype(BF16)
    r1 = v - p1.astype(F32)
    p2 = r1.astype(BF16)
    p3 = (r1 - p2.astype(F32)).astype(BF16)
    return p1, p2, p3


def _full(shape):
    return pl.BlockSpec(shape, lambda *_: (0,) * len(shape))


def _conv_silu(cbuf, new_rows, out_ref, convw_ref, convb_ref):
    n, width = new_rows.shape
    cbuf[SUBLANES:SUBLANES + n, :] = new_rows
    for c0 in range(0, width, LANES):
        cols = slice(c0, c0 + LANES)
        full = cbuf[:, cols]
        acc = jnp.broadcast_to(convb_ref[:, cols], (n, LANES))
        for i in range(CONV_W):
            back = CONV_W - 1 - i
            shifted = full if back == 0 else pltpu.roll(full, back, axis=0)
            acc = acc + shifted[SUBLANES:SUBLANES + n, :] * convw_ref[i:i + 1, cols]
        out_ref[:, cols] = _silu(acc)
    cbuf[0:SUBLANES, :] = cbuf[n:n + SUBLANES, :]


def _in_proj_kernel(x_ref, wz_ref, wxbc_ref, wdt_ref, wq_ref, wk_ref, wv_ref,
                    z_ref, xbc_ref, dt_ref, q_ref, k_ref, v_ref, *, q_scale):
    xb = x_ref[...].astype(BF16)
    z_ref[...] = jnp.dot(xb, wz_ref[...], preferred_element_type=F32)
    xbc_ref[...] = jnp.dot(xb, wxbc_ref[...], preferred_element_type=F32)
    dt_ref[...] = jnp.dot(xb, wdt_ref[...], preferred_element_type=F32)
    q_ref[...] = (jnp.dot(xb, wq_ref[...], preferred_element_type=F32) * q_scale).astype(BF16)
    k_ref[...] = jnp.dot(xb, wk_ref[...], preferred_element_type=F32).astype(BF16)
    v_ref[...] = jnp.dot(xb, wv_ref[...], preferred_element_type=F32).astype(BF16)


def _in_proj(x2d, w, tm):
    n, d = x2d.shape
    d_ssm, d_conv, d_att = w['wz'].shape[1], w['wxbc'].shape[1], w['wq'].shape[1]
    row = lambda width: pl.BlockSpec((tm, width), lambda i: (i, 0))
    return pl.pallas_call(
        functools.partial(_in_proj_kernel, q_scale=w['q_scale']),
        grid=(n // tm,),
        in_specs=[row(d), _full(w['wz'].shape), _full(w['wxbc'].shape), _full(w['wdt'].shape),
                  _full(w['wq'].shape), _full(w['wk'].shape), _full(w['wv'].shape)],
        out_specs=[row(d_ssm), row(d_conv), row(LANES), row(d_att), row(d_att), row(d_att)],
        out_shape=[jax.ShapeDtypeStruct((n, d_ssm), F32), jax.ShapeDtypeStruct((n, d_conv), F32),
                   jax.ShapeDtypeStruct((n, LANES), F32), jax.ShapeDtypeStruct((n, d_att), BF16),
                   jax.ShapeDtypeStruct((n, d_att), BF16), jax.ShapeDtypeStruct((n, d_att), BF16)],
        compiler_params=_cparams(("parallel",)),
        name="in_proj",
    )(x2d, w['wz'], w['wxbc'], w['wdt'], w['wq'], w['wk'], w['wv'])


def _ssd_kernel(xbc_ref, dt_ref, cprev_ref, h0_ref, convw_ref, convb_ref, dtb_ref, alog_ref,
                dskip_ref, tril_ref, ex_ref, y_ref, hout_ref, cbuf, hst, xc_s, dt_s,
                *, blk, sub, d_ssm, d_state, head_dim):
    c = pl.program_id(1)
    gw = d_ssm // SSM_GROUPS

    @pl.when(c == 0)
    def _():
        hst[...] = h0_ref[0]
        cbuf[0:SUBLANES, :] = cprev_ref[0]

    _conv_silu(cbuf, xbc_ref[0], xc_s, convw_ref, convb_ref)

    dtr = dt_ref[0] + dtb_ref[...]
    dt_s[...] = jnp.maximum(dtr, 0.0) + jnp.log1p(jnp.exp(-jnp.abs(dtr)))
    a = -jnp.exp(alog_ref[...])

    row = lax.broadcasted_iota(jnp.int32, (sub, sub), 0)
    col = lax.broadcasted_iota(jnp.int32, (sub, sub), 1)
    causal = row >= col
    lane = lax.broadcasted_iota(jnp.int32, (sub, LANES), 1)
    first_head = lane < head_dim
    assert 2 * head_dim == LANES
    pairs_per_group = gw // LANES

    for sc in range(blk // sub):
        r0 = sc * sub
        dt = dt_s[r0:r0 + sub, :]
        acs = sum(jnp.dot(tril_ref[...], piece, preferred_element_type=F32) for piece in _split3(dt * a))
        acs_t = acs.T
        acs_p, dt_p = _split3(acs), _split3(dt)
        for g in range(SSM_GROUPS):
            gs = slice(g * gw, (g + 1) * gw)
            xs = xc_s[r0:r0 + sub, gs]
            acs_e = sum(jnp.dot(piece, ex_ref[:, gs], preferred_element_type=F32) for piece in acs_p)
            dt_e = sum(jnp.dot(piece, ex_ref[:, gs], preferred_element_type=F32) for piece in dt_p)
            last_e = acs_e[sub - 1:sub, :]
            dtx = dt_e * xs
            xw = (jnp.exp(last_e - acs_e) * dtx).astype(BF16)
            dtx = dtx.astype(BF16)
            eacs_e = jnp.exp(acs_e)
            b0 = d_ssm + g * d_state
            c0 = d_ssm + (SSM_GROUPS + g) * d_state
            bm = xc_s[r0:r0 + sub, b0:b0 + d_state].astype(BF16)
            cm = xc_s[r0:r0 + sub, c0:c0 + d_state].astype(BF16)
            hg = hst[g]
            cb = lax.dot_general(cm, bm, (((1,), (1,)), ((), ())), preferred_element_type=F32)
            y_off = jnp.dot(cm, hg.astype(BF16), preferred_element_type=F32)
            for jp in range(pairs_per_group):
                lg = slice(jp * LANES, (jp + 1) * LANES)
                p0 = g * gw + jp * LANES
                h0 = p0 // head_dim
                xpair = dtx[:, lg]
                outs = []
                for hh in (h0, h0 + 1):
                    seg = acs[:, hh:hh + 1] - acs_t[hh:hh + 1, :]
                    dec = jnp.exp(jnp.where(causal, seg, NEG_BIG))
                    outs.append(jnp.dot((cb * dec).astype(BF16), xpair, preferred_element_type=F32))
                y_diag = jnp.where(first_head, outs[0], outs[1])
                sl = slice(p0, p0 + LANES)
                y_ref[0, r0:r0 + sub, sl] = y_diag + y_off[:, lg] * eacs_e[:, lg] + dskip_ref[:, sl] * xs[:, lg]
            st = lax.dot_general(bm, xw, (((0,), (0,)), ((), ())), preferred_element_type=F32)
            hst[g] = hg * jnp.exp(last_e) + st

    @pl.when(c == pl.num_programs(1) - 1)
    def _():
        hout_ref[0] = hst[...]


def _ssd(xbc, dt, cprev8, h0t, p, blk):
    b, t, d_conv = xbc.shape
    d_ssm = p['dskip_e'].shape[1]
    d_state = (d_conv - d_ssm) // (2 * SSM_GROUPS)
    gw = d_ssm // SSM_GROUPS
    sub = min(blk, SSD_SUB)
    tril = jnp.tril(jnp.ones((sub, sub), BF16))
    kern = functools.partial(_ssd_kernel, blk=blk, sub=sub, d_ssm=d_ssm, d_state=d_state,
                             head_dim=d_ssm // p['a_log_heads'])
    return pl.pallas_call(
        kern,
        grid=(b, t // blk),
        in_specs=[pl.BlockSpec((1, blk, d_conv), lambda i, c: (i, c, 0)),
                  pl.BlockSpec((1, blk, LANES), lambda i, c: (i, c, 0)),
                  pl.BlockSpec((1, SUBLANES, d_conv), lambda i, c: (i, 0, 0)),
                  pl.BlockSpec((1, SSM_GROUPS, d_state, gw), lambda i, c: (i, 0, 0, 0)),
                  _full(p['conv_w'].shape), _full(p['conv_b'].shape), _full(p['dt_bias'].shape),
                  _full(p['a_log'].shape), _full(p['dskip_e'].shape), _full(tril.shape),
                  _full(p['expand'].shape)],
        out_specs=[pl.BlockSpec((1, blk, d_ssm), lambda i, c: (i, c, 0)),
                   pl.BlockSpec((1, SSM_GROUPS, d_state, gw), lambda i, c: (i, 0, 0, 0))],
        out_shape=[jax.ShapeDtypeStruct((b, t, d_ssm), F32),
                   jax.ShapeDtypeStruct((b, SSM_GROUPS, d_state, gw), F32)],
        scratch_shapes=[pltpu.VMEM((blk + SUBLANES, d_conv), F32),
                        pltpu.VMEM((SSM_GROUPS, d_state, gw), F32),
                        pltpu.VMEM((blk, d_conv), F32), pltpu.VMEM((blk, LANES), F32)],
        compiler_params=_cparams(("parallel", "arbitrary")),
        name="ssd",
    )(xbc, dt, cprev8, h0t, p['conv_w'], p['conv_b'], p['dt_bias'], p['a_log'], p['dskip_e'], tril,
      p['expand'])


def _attn_kernel(q_ref, kp_ref, kc_ref, vp_ref, vc_ref, bias_ref, o_ref, kwin, vwin,
                 *, past, qb, sub, kw, heads, head_dim, mask_positions):
    c = pl.program_id(1)
    kwin[0:past, :] = kp_ref[0]
    kwin[past:past + qb, :] = kc_ref[0]
    vwin[0:past, :] = vp_ref[0]
    vwin[past:past + qb, :] = vc_ref[0]
    lane = lax.broadcasted_iota(jnp.int32, (sub, LANES), 1)
    first_head = lane < head_dim

    def sub_block(i, masked):
        r0 = pl.multiple_of(i * sub, sub)
        if masked:
            col = lax.broadcasted_iota(jnp.int32, (2 * sub, kw), 1)
            valid = (col + (c * qb - past + i * sub)) >= 0
        for jp in range(heads * head_dim // LANES):
            ls = slice(jp * LANES, (jp + 1) * LANES)
            q2 = q_ref[0, pl.ds(r0, sub), ls]
            k2 = kwin[pl.ds(r0, kw), ls]
            v2 = vwin[pl.ds(r0, kw), ls]
            zero = jnp.zeros_like(q2)
            qq = jnp.concatenate([jnp.where(first_head, q2, zero), jnp.where(first_head, zero, q2)], axis=0)
            s = lax.dot_general(qq, k2, (((1,), (1,)), ((), ())), preferred_element_type=F32)
            s = s + bias_ref[jp]
            if masked:
                s = jnp.where(valid, s, NEG_BIG)
            m = jnp.max(s, axis=-1, keepdims=True)
            e = jnp.exp(s - m)
            l = jnp.sum(e, axis=-1, keepdims=True)
            o = jnp.dot(e.astype(BF16), v2, preferred_element_type=F32) / l
            o_ref[0, pl.ds(r0, sub), ls] = jnp.where(first_head, o[:sub], o[sub:])

    def run(masked):
        def body(i, carry):
            sub_block(i, masked)
            return carry
        lax.fori_loop(0, qb // sub, body, 0)

    if mask_positions:
        pl.when(c == 0)(lambda: run(True))
        pl.when(c > 0)(lambda: run(False))
    else:
        run(False)


def _attention(q, k, v, k_past, v_past, bias, *, qb, sub, mask_positions, heads):
    b, t, d = q.shape
    kw = bias.shape[2]
    bias = bias.reshape(heads // 2, 2 * sub, kw)
    if k_past is None:
        past = qb
        prev_spec = pl.BlockSpec((1, past, d), lambda i, c: (i, jnp.maximum(c - 1, 0), 0))
        k_past, v_past = k, v
    else:
        assert t == qb
        past = k_past.shape[1]
        prev_spec = pl.BlockSpec((1, past, d), lambda i, c: (i, 0, 0))
    cur_spec = pl.BlockSpec((1, qb, d), lambda i, c: (i, c, 0))
    kern = functools.partial(_attn_kernel, past=past, qb=qb, sub=sub, kw=kw, heads=heads,
                             head_dim=d // heads, mask_positions=mask_positions)
    return pl.pallas_call(
        kern,
        grid=(b, t // qb),
        in_specs=[cur_spec, prev_spec, cur_spec, prev_spec, cur_spec, _full(bias.shape)],
        out_specs=pl.BlockSpec((1, qb, d), lambda i, c: (i, c, 0)),
        out_shape=jax.ShapeDtypeStruct((b, t, d), F32),
        scratch_shapes=[pltpu.VMEM((past + qb, d), BF16), pltpu.VMEM((past + qb, d), BF16)],
        compiler_params=_cparams(("parallel", "arbitrary")),
        name="attention",
    )(q, k_past, k, v_past, v, bias)


def _band_bias(table, sub, kw, chunked):
    assert kw - sub == BAND_PAST or not chunked
    p_len = sub + kw
    dist = (kw - sub) + (sub - 1) - jnp.arange(p_len)
    vec = table[jnp.clip(dist, -REL_CLIP, REL_CLIP) + REL_CLIP].T.astype(F32)
    skew = jnp.tile(vec, (1, sub))[:, :sub * (p_len - 1)].reshape(-1, sub, p_len - 1)
    bias = skew[:, :, sub - 1:sub - 1 + kw]
    if chunked:
        s_loc = jnp.arange(kw)[None, :] - (jnp.arange(sub)[:, None] // CHUNK) * CHUNK
        bias = jnp.where(((s_loc >= 0) & (s_loc < BAND_PAST + CHUNK))[None], bias, NEG_BIG)
    return bias


def _layer_norm(r, g, b):
    mu = jnp.mean(r, axis=-1, keepdims=True)
    d = r - mu
    var = jnp.mean(d * d, axis=-1, keepdims=True)
    return d * lax.rsqrt(var + NORM_EPS) * g + b


def _post_mixer_kernel(y_ref, z_ref, o_ref, x_ref, wo_ref, gs_ref, ga_ref, lg_ref, lb_ref, rwt_ref, rb_ref,
                       hx_ref, cscr, *, alpha, n_experts, d_ssm):
    u = y_ref[...] * _silu(z_ref[...])
    ys = u * lax.rsqrt(jnp.mean(u * u, axis=-1, keepdims=True) + NORM_EPS) * gs_ref[...]
    o = o_ref[...]
    oa = o * lax.rsqrt(jnp.mean(o * o, axis=-1, keepdims=True) + NORM_EPS) * ga_ref[...]
    mix = (jnp.dot(ys.astype(BF16), wo_ref[0:d_ssm, :], preferred_element_type=F32)
           + jnp.dot(oa.astype(BF16), wo_ref[d_ssm:, :], preferred_element_type=F32))
    h = _layer_norm(alpha * x_ref[...] + mix, lg_ref[...], lb_ref[...])

    epg = EXPERTS_PER_GROUP
    assert n_experts == N_EXPERT_GROUPS * epg and epg == 4 and N_EXPERT_GROUPS <= SUBLANES
    h0 = h.astype(BF16)
    h1 = (h - h0.astype(F32)).astype(BF16)
    contract_last = (((1,), (1,)), ((), ()))
    lt0 = lax.dot_general(rwt_ref[...], h0, contract_last, preferred_element_type=F32)
    lt1 = lax.dot_general(rwt_ref[...], h1, contract_last, preferred_element_type=F32)
    rr = ROUTER_ROWS
    logits = lt0[0:rr] + lt0[rr:2 * rr] + lt0[2 * rr:3 * rr] + lt1[0:rr] + lt1[rr:2 * rr]
    grp = lax.broadcasted_iota(jnp.int32, (SUBLANES, logits.shape[1]), 0)
    real = grp < N_EXPERT_GROUPS
    lg = [jnp.where(real, logits[SUBLANES * j:SUBLANES * (j + 1)], -jnp.inf) for j in range(epg)]
    mx = jnp.max(functools.reduce(jnp.maximum, lg), axis=0, keepdims=True)
    ex = [jnp.exp(v - mx) for v in lg]
    zsum = jnp.sum(functools.reduce(lambda p, q: p + q, ex), axis=0, keepdims=True)
    score = [e / zsum for e in ex]
    sel = [score[j] + rb_ref[SUBLANES * j:SUBLANES * (j + 1), :] for j in range(epg)]

    def top2_sum(v):
        a, b = jnp.maximum(v[0], v[1]), jnp.minimum(v[0], v[1])
        c, d = jnp.maximum(v[2], v[3]), jnp.minimum(v[2], v[3])
        return jnp.maximum(a, c) + jnp.maximum(jnp.minimum(a, c), jnp.maximum(b, d))

    gscore = jnp.where(real, top2_sum(sel), -jnp.inf)
    gbest = jnp.max(gscore, axis=0, keepdims=True)
    gidx = jnp.min(jnp.where(gscore == gbest, grp, SUBLANES), axis=0, keepdims=True)
    chosen = grp == gidx
    in_sel = [jnp.sum(jnp.where(chosen, v, 0.0), axis=0, keepdims=True) for v in sel]
    in_score = [jnp.sum(jnp.where(chosen, v, 0.0), axis=0, keepdims=True) for v in score]

    def argmax_first(vals, exclude=None):
        bv, bi = None, None
        for j, v in enumerate(vals):
            if exclude is not None:
                v = jnp.where(exclude == j, -jnp.inf, v)
            if bv is None:
                bv, bi = v, jnp.zeros_like(gidx)
            else:
                upd = v > bv
                bv = jnp.where(upd, v, bv)
                bi = jnp.where(upd, j, bi)
        return bi

    j1 = argmax_first(in_sel)
    j2 = argmax_first(in_sel, exclude=j1)
    s1 = functools.reduce(lambda p, q: p + q, [jnp.where(j1 == j, in_score[j], 0.0) for j in range(epg)])
    s2 = functools.reduce(lambda p, q: p + q, [jnp.where(j2 == j, in_score[j], 0.0) for j in range(epg)])
    g1, g2 = s1 / (s1 + s2), s2 / (s1 + s2)
    first_lo = j1 < j2
    lo, hi = jnp.where(first_lo, j1, j2), jnp.where(first_lo, j2, j1)
    pair = jnp.where(lo == 0, hi - 1, jnp.where(lo == 1, hi + 1, PAIRS_PER_GROUP - 1))
    bucket = gidx * PAIRS_PER_GROUP + pair
    cscr[...] = jnp.zeros_like(cscr)
    cscr[ROUTE_BUCKET:ROUTE_BUCKET + 1, :] = bucket.astype(F32)
    cscr[ROUTE_GATE_LO:ROUTE_GATE_LO + 1, :] = jnp.where(first_lo, g1, g2)
    cscr[ROUTE_GATE_HI:ROUTE_GATE_HI + 1, :] = jnp.where(first_lo, g2, g1)
    d = h.shape[1]
    hx_ref[:, :d] = h
    hx_ref[:, d:] = cscr[...].T


def _post_mixer(y, z, o, x, p, tm, alpha):
    n, d = x.shape
    d_ssm = y.shape[1]
    row = lambda width: pl.BlockSpec((tm, width), lambda i: (i, 0))
    n_experts = p['n_experts']
    kern = functools.partial(_post_mixer_kernel, alpha=alpha, n_experts=n_experts, d_ssm=d_ssm)
    return pl.pallas_call(
        kern,
        grid=(n // tm,),
        in_specs=[row(d_ssm), row(d_ssm), row(o.shape[1]), row(d), _full(p['w_out'].shape),
                  _full(p['ssd_norm_g'].shape), _full(p['attn_norm_g'].shape), _full(p['ln1_g'].shape),
                  _full(p['ln1_b'].shape), _full(p['router_wt'].shape), _full(p['router_bias'].shape)],
        out_specs=row(d + LANES),
        out_shape=jax.ShapeDtypeStruct((n, d + LANES), F32),
        scratch_shapes=[pltpu.VMEM((LANES, tm), F32)],
        compiler_params=_cparams(("parallel",)),
        name="post_mixer",
    )(y, z, o, x, p['w_out'], p['ssd_norm_g'], p['attn_norm_g'], p['ln1_g'], p['ln1_b'], p['router_wt'],
      p['router_bias'])


def _start_row_gather(idx_ref, tile, src_hbm, buf, sem, slot, rows):
    for r in range(rows):
        row = idx_ref[tile, r]
        pltpu.make_async_copy(src_hbm.at[pl.ds(row, 1), :], buf.at[slot, pl.ds(r, 1), :], sem.at[slot]).start()


def _wait_row_gather(src_hbm, buf, sem, slot, rows):
    pltpu.make_async_copy(src_hbm.at[pl.ds(0, rows), :], buf.at[slot], sem.at[slot]).wait()


def _prefetched_rows(idx_ref, src_hbm, buf, sem, rows):
    i = pl.program_id(0)
    slot = lax.rem(i, 2)

    @pl.when(i == 0)
    def _():
        _start_row_gather(idx_ref, 0, src_hbm, buf, sem, 0, rows)

    @pl.when(i + 1 < pl.num_programs(0))
    def _():
        _start_row_gather(idx_ref, i + 1, src_hbm, buf, sem, 1 - slot, rows)

    _wait_row_gather(src_hbm, buf, sem, slot, rows)
    return slot


def _moe_kernel(src_ref, elo_ref, ehi_ref, nused_ref, hx_hbm, wg_lo, wu_lo, wd_lo, wg_hi, wu_hi, wd_hi,
                y_ref, xbuf, sem, *, tm, d):
    del elo_ref, ehi_ref
    slot = _prefetched_rows(src_ref, hx_hbm, xbuf, sem, tm)

    @pl.when(pl.program_id(0) < nused_ref[0])
    def _():
        x = xbuf[slot]
        xb = x[:, :d].astype(BF16)

        def ffn(wg, wu, wd):
            gate = jnp.dot(xb, wg[0], preferred_element_type=F32)
            up = jnp.dot(xb, wu[0], preferred_element_type=F32)
            return jnp.dot((_silu(gate) * up).astype(BF16), wd[0], preferred_element_type=F32)

        y_ref[...] = (x[:, d + ROUTE_GATE_LO:d + ROUTE_GATE_LO + 1] * ffn(wg_lo, wu_lo, wd_lo)
                      + x[:, d + ROUTE_GATE_HI:d + ROUTE_GATE_HI + 1] * ffn(wg_hi, wu_hi, wd_hi))

    @pl.when(pl.program_id(0) >= nused_ref[0])
    def _():
        y_ref[...] = jnp.zeros_like(y_ref)


def _moe(hx, src2d, elo, ehi, nused, p, d):
    n_tiles, tm = src2d.shape
    _, _, d_exp = p['w_gate'].shape
    wspec = lambda sel: pl.BlockSpec((1, d, d_exp), lambda i, s, lo, hi, nu: ((lo if sel == 0 else hi)[i], 0, 0))
    wdspec = lambda sel: pl.BlockSpec((1, d_exp, d), lambda i, s, lo, hi, nu: ((lo if sel == 0 else hi)[i], 0, 0))
    return pl.pallas_call(
        functools.partial(_moe_kernel, tm=tm, d=d),
        grid_spec=pltpu.PrefetchScalarGridSpec(
            num_scalar_prefetch=4,
            grid=(n_tiles,),
            in_specs=[pl.BlockSpec(memory_space=pl.ANY), wspec(0), wspec(0), wdspec(0),
                      wspec(1), wspec(1), wdspec(1)],
            out_specs=pl.BlockSpec((tm, d), lambda i, *_: (i, 0)),
            scratch_shapes=[pltpu.VMEM((2, tm, hx.shape[1]), F32), pltpu.SemaphoreType.DMA((2,))]),
        out_shape=jax.ShapeDtypeStruct((n_tiles * tm, d), F32),
        compiler_params=_cparams(("arbitrary",)),
        name="moe",
    )(src2d, elo, ehi, nused, hx, p['w_gate'], p['w_up'], p['w_down'], p['w_gate'], p['w_up'], p['w_down'])


def _finalize_kernel(pos_ref, y_hbm, hx_ref, lg_ref, lb_ref, out_ref, ybuf, sem, *, tm, d, alpha):
    slot = _prefetched_rows(pos_ref, y_hbm, ybuf, sem, tm)
    out_ref[...] = _layer_norm(alpha * hx_ref[:, :d] + ybuf[slot], lg_ref[...], lb_ref[...])


def _finalize(y_sorted, pos2d, hx, p, d, alpha):
    n_tiles, tm = pos2d.shape
    return pl.pallas_call(
        functools.partial(_finalize_kernel, tm=tm, d=d, alpha=alpha),
        grid_spec=pltpu.PrefetchScalarGridSpec(
            num_scalar_prefetch=1,
            grid=(n_tiles,),
            in_specs=[pl.BlockSpec(memory_space=pl.ANY),
                      pl.BlockSpec((tm, hx.shape[1]), lambda i, pos: (i, 0)),
                      pl.BlockSpec(p['ln2_g'].shape, lambda i, pos: (0, 0)),
                      pl.BlockSpec(p['ln2_b'].shape, lambda i, pos: (0, 0))],
            out_specs=pl.BlockSpec((tm, d), lambda i, pos: (i, 0)),
            scratch_shapes=[pltpu.VMEM((2, tm, d), F32), pltpu.SemaphoreType.DMA((2,))]),
        out_shape=jax.ShapeDtypeStruct((n_tiles * tm, d), F32),
        compiler_params=_cparams(("arbitrary",)),
        name="finalize",
    )(pos2d, y_sorted, hx, p['ln2_g'], p['ln2_b'])


def _route_tables(bucket, n_groups, tm):
    n = bucket.shape[0]
    n_buckets = n_groups * PAIRS_PER_GROUP
    n_tiles = n // tm + n_buckets
    onehot = bucket.reshape(n // tm, tm)[:, :, None] == jnp.arange(n_buckets, dtype=jnp.int32)
    tril = jnp.tril(jnp.ones((tm, tm), BF16))
    within = jnp.einsum('ij,tjb->tib', tril, onehot.astype(BF16), preferred_element_type=F32)
    block_counts = within[:, -1, :]
    before = jnp.cumsum(block_counts, axis=0) - block_counts
    counts = (before[-1] + block_counts[-1]).astype(jnp.int32)
    tiles_per = (counts + tm - 1) // tm
    tile_end = jnp.cumsum(tiles_per)
    start = ((tile_end - tiles_per) * tm).astype(F32)
    pos = jnp.sum(jnp.where(onehot, within - 1.0 + before[:, None, :] + start, 0.0), axis=-1)
    pos = pos.reshape(n).astype(jnp.int32)
    src = jnp.zeros((n_tiles * tm,), jnp.int32).at[pos].set(jnp.arange(n, dtype=jnp.int32), unique_indices=True,
                                                            mode='promise_in_bounds')
    tile_bucket = jnp.sum(tile_end[None, :] <= jnp.arange(n_tiles, dtype=jnp.int32)[:, None], axis=1)
    tile_bucket = jnp.minimum(tile_bucket, n_buckets - 1).astype(jnp.int32)
    group, pair = tile_bucket // PAIRS_PER_GROUP, tile_bucket % PAIRS_PER_GROUP
    elo = group * EXPERTS_PER_GROUP + jnp.asarray([lo for lo, _ in PAIRS], jnp.int32)[pair]
    ehi = group * EXPERTS_PER_GROUP + jnp.asarray([hi for _, hi in PAIRS], jnp.int32)[pair]
    return pos, src.reshape(n_tiles, tm), elo, ehi, tile_end[-1:].astype(jnp.int32)


def _pick(n, candidates):
    for c in candidates:
        if n % c == 0:
            return c
    raise ValueError(f"no tile size for {n}")


def _router_rows(per_expert):
    a = per_expert.reshape(N_EXPERT_GROUPS, EXPERTS_PER_GROUP, -1).swapaxes(0, 1)
    a = jnp.pad(a, ((0, 0), (0, SUBLANES - N_EXPERT_GROUPS), (0, 0)))
    return a.reshape(ROUTER_ROWS, -1)


def _layer_params(i, w_in, conv_w, conv_b, dt_bias, a_log, d_skip, ssd_norm_g, attn_norm_g, w_out,
                  ln1_g, ln1_b, router_w, router_bias, w_gate, w_up, w_down, ln2_g, ln2_b, dims):
    d_ssm, d_conv, heads, d_att, att_heads = dims
    o = 0
    wi = w_in[i]
    wz = wi[:, o:o + d_ssm]; o += d_ssm
    wxbc = wi[:, o:o + d_conv]; o += d_conv
    wdt = wi[:, o:o + heads]; o += heads
    wq = wi[:, o:o + d_att]; o += d_att
    wk = wi[:, o:o + d_att]; o += d_att
    wv = wi[:, o:o + d_att]
    pad_lanes = lambda v: jnp.pad(v, ((0, 0), (0, LANES - v.shape[1])))
    head_dim = d_ssm // heads
    expand = (jnp.arange(LANES)[:, None] == (jnp.arange(d_ssm)[None, :] // head_dim)).astype(BF16)
    return {
        'wz': wz.astype(BF16), 'wxbc': wxbc.astype(BF16), 'wdt': pad_lanes(wdt).astype(BF16),
        'wq': wq.astype(BF16), 'wk': wk.astype(BF16), 'wv': wv.astype(BF16),
        'q_scale': float(d_att // att_heads) ** -0.5,
        'conv_w': conv_w[i], 'conv_b': conv_b[i][None, :],
        'dt_bias': pad_lanes(dt_bias[i][None, :]), 'a_log': pad_lanes(a_log[i][None, :]),
        'dskip_e': jnp.repeat(d_skip[i], head_dim)[None, :], 'expand': expand, 'a_log_heads': heads,
        'ssd_norm_g': ssd_norm_g[i][None, :], 'attn_norm_g': attn_norm_g[i][None, :],
        'w_out': w_out[i].astype(BF16), 'ln1_g': ln1_g[i][None, :], 'ln1_b': ln1_b[i][None, :],
        'router_wt': jnp.pad(jnp.concatenate([_router_rows(piece) for piece in _split3(router_w.T)]),
                             ((0, LANES - 3 * ROUTER_ROWS), (0, 0))),
        'router_bias': _router_rows(router_bias[:, None]), 'n_experts': router_bias.shape[0],
        'w_gate': w_gate[i].astype(BF16), 'w_up': w_up[i].astype(BF16), 'w_down': w_down[i].astype(BF16),
        'ln2_g': ln2_g[i][None, :], 'ln2_b': ln2_b[i][None, :],
    }


def _trunk(x, layers, biases, conv_prev, h0, k_prev, v_prev, alpha, heads):
    b, t, d = x.shape
    n = b * t
    prompt = k_prev is None
    ks, vs, hs, cs = [], [], [], []
    x2 = x.reshape(n, d)
    tm = _pick(n, (512, 256, 128, 64, 32, 16, 8))
    tm_fin = _pick(n, (512, 256))
    n_buckets = N_EXPERT_GROUPS * PAIRS_PER_GROUP
    moe_tile = MOE_TILE if n >= 2 * n_buckets * MOE_TILE else MOE_TILE_SMALL
    assert n % moe_tile == 0
    for i, p in enumerate(layers):
        z, xbc, dt, q, k, v = _in_proj(x2, p, tm)
        d_conv = xbc.shape[1]
        d_ssm = z.shape[1]
        d_state = (d_conv - d_ssm) // (2 * SSM_GROUPS)
        xbc3 = xbc.reshape(b, t, d_conv)
        cprev8 = jnp.pad(conv_prev[i], ((0, 0), (SUBLANES - (CONV_W - 1), 0), (0, 0)))
        h0t = jnp.swapaxes(h0[i].reshape(b, SSM_GROUPS, d_ssm // SSM_GROUPS, d_state), 2, 3)
        blk = _pick(t, (512, 256, 128, 64, 32, 16, 8))
        y, ht = _ssd(xbc3, dt.reshape(b, t, LANES), cprev8, h0t, p, blk)
        q3, k3, v3 = (a.reshape(b, t, -1) for a in (q, k, v))
        if prompt:
            o = _attention(q3, k3, v3, None, None, biases[i], qb=BAND_PAST, sub=ATT_SUB,
                           mask_positions=True, heads=heads)
            keep = min(BAND_PAST, t)
            k_state, v_state = k3[:, t - keep:], v3[:, t - keep:]
        else:
            o = _attention(q3, k3, v3, k_prev[i], v_prev[i], biases[i], qb=t, sub=t,
                           mask_positions=False, heads=heads)
            k_state, v_state = k3, v3
        hx = _post_mixer(y.reshape(n, d_ssm), z, o.reshape(n, -1), x2, p, tm, alpha)
        bucket = hx[:, d + ROUTE_BUCKET].astype(jnp.int32)
        pos, src2d, elo, ehi, nused = _route_tables(bucket, N_EXPERT_GROUPS, moe_tile)
        y_sorted = _moe(hx, src2d, elo, ehi, nused, p, d)
        x2 = _finalize(y_sorted, pos.reshape(n // tm_fin, tm_fin), hx, p, d, alpha)
        head_dim = k3.shape[2] // heads
        ks.append(k_state.astype(F32).reshape(b, -1, heads, head_dim))
        vs.append(v_state.astype(F32).reshape(b, -1, heads, head_dim))
        hs.append(jnp.swapaxes(ht, 2, 3).reshape(h0[i].shape))
        full = jnp.concatenate([conv_prev[i], xbc3], axis=1) if t < CONV_W - 1 else xbc3
        cs.append(full[:, full.shape[1] - (CONV_W - 1):])
    return x2.reshape(b, t, d), jnp.stack(ks), jnp.stack(vs), jnp.stack(hs), jnp.stack(cs)


def kernel(x_prompt, x_sample, cache_k, cache_v, state_ssm, state_conv, w_in, conv_w, conv_b, dt_bias, a_log,
           d_skip, ssd_norm_g, attn_norm_g, rel_bias, w_out, ln1_g, ln1_b, router_w, router_bias, w_gate, w_up,
           w_down, ln2_g, ln2_b):
    depth = w_in.shape[0]
    heads = cache_k.shape[3]
    d_att = heads * cache_k.shape[4]
    d_conv = state_conv.shape[3]
    ssm_heads = state_ssm.shape[2]
    d_ssm = ssm_heads * state_ssm.shape[3]
    alpha = (2 * depth) ** 0.25
    dims = (d_ssm, d_conv, ssm_heads, d_att, heads)
    layers = [_layer_params(i, w_in, conv_w, conv_b, dt_bias, a_log, d_skip, ssd_norm_g, attn_norm_g, w_out,
                            ln1_g, ln1_b, router_w, router_bias, w_gate, w_up, w_down, ln2_g, ln2_b, dims)
              for i in range(depth)]
    bp, tp, _ = x_prompt.shape
    bs, ts, _ = x_sample.shape

    conv0 = jnp.zeros((depth, bp, CONV_W - 1, d_conv), F32)
    h00 = jnp.zeros((depth, bp) + state_ssm.shape[2:], F32)
    bias_p = [_band_bias(rel_bias[i], ATT_SUB, BAND_PAST + ATT_SUB, True) for i in range(depth)]
    y_p, k_p, v_p, h_p, c_p = _trunk(x_prompt, layers, bias_p, conv0, h00, None, None, alpha, heads)

    past = cache_k.shape[2]
    bias_s = [_band_bias(rel_bias[i], ts, past + ts, False) for i in range(depth)]
    ck = cache_k.reshape(depth, bs, past, d_att).astype(BF16)
    cv = cache_v.reshape(depth, bs, past, d_att).astype(BF16)
    y_s, k_s, v_s, h_s, c_s = _trunk(x_sample, layers, bias_s, state_conv, state_ssm, ck, cv, alpha, heads)
    return (y_p, y_s, k_p, v_p, h_p, c_p, k_s, v_s, h_s, c_s)
```

```python
import functools

import jax
import jax.numpy as jnp
from jax import lax
from jax.experimental import pallas as pl
from jax.experimental.pallas import tpu as pltpu

F32 = jnp.float32
BF16 = jnp.bfloat16

CHUNK = 64
BAND_PREV_CHUNKS = 8
BAND_PAST = BAND_PREV_CHUNKS * CHUNK
REL_CLIP = 256
SSM_GROUPS = 2
CONV_W = 4
N_EXPERT_GROUPS = 4
EXPERTS_PER_GROUP = 4
PAIRS = [(lo, hi) for lo in range(EXPERTS_PER_GROUP) for hi in range(lo + 1, EXPERTS_PER_GROUP)]
PAIRS_PER_GROUP = len(PAIRS)
ROUTE_BUCKET, ROUTE_GATE_LO, ROUTE_GATE_HI = 0, 1, 2
MOE_TILE = 256
MOE_TILE_SMALL = 64
ATT_SUB = 4 * CHUNK
SSD_SUB = 128
NORM_EPS = 1e-5
NEG_BIG = -1e30

LANES = 128
SUBLANES = 8
VMEM_LIMIT = 56 * 1024 * 1024
ROUTER_ROWS = SUBLANES * EXPERTS_PER_GROUP


def _cparams(sem):
    return pltpu.CompilerParams(dimension_semantics=sem, vmem_limit_bytes=VMEM_LIMIT)


def _silu(v):
    half = 0.5 * v
    return half + half * jnp.tanh(half)


def _split3(v):
    p1 = v.astype(BF16)
    r1 = v - p1.astype(F32)
    p2 = r1.astype(BF16)
    p3 = (r1 - p2.astype(F32)).astype(BF16)
    return p1, p2, p3


def _full(shape):
    return pl.BlockSpec(shape, lambda *_: (0,) * len(shape))


def _conv_silu(cbuf, new_rows, out_ref, convw_ref, convb_ref):
    n, width = new_rows.shape
    cbuf[SUBLANES:SUBLANES + n, :] = new_rows
    for c0 in range(0, width, LANES):
        cols = slice(c0, c0 + LANES)
        full = cbuf[:, cols]
        acc = jnp.broadcast_to(convb_ref[:, cols], (n, LANES))
        for i in range(CONV_W):
            back = CONV_W - 1 - i
            shifted = full if back == 0 else pltpu.roll(full, back, axis=0)
            acc = acc + shifted[SUBLANES:SUBLANES + n, :] * convw_ref[i:i + 1, cols]
        out_ref[:, cols] = _silu(acc)
    cbuf[0:SUBLANES, :] = cbuf[n:n + SUBLANES, :]


def _in_proj_kernel(x_ref, wz_ref, wxbc_ref, wdt_ref, wq_ref, wk_ref, wv_ref,
                    z_ref, xbc_ref, dt_ref, q_ref, k_ref, v_ref, *, q_scale):
    xb = x_ref[...].astype(BF16)
    z_ref[...] = jnp.dot(xb, wz_ref[...], preferred_element_type=F32)
    xbc_ref[...] = jnp.dot(xb, wxbc_ref[...], preferred_element_type=F32)
    dt_ref[...] = jnp.dot(xb, wdt_ref[...], preferred_element_type=F32)
    q_ref[...] = (jnp.dot(xb, wq_ref[...], preferred_element_type=F32) * q_scale).astype(BF16)
    k_ref[...] = jnp.dot(xb, wk_ref[...], preferred_element_type=F32).astype(BF16)
    v_ref[...] = jnp.dot(xb, wv_ref[...], preferred_element_type=F32).astype(BF16)


def _in_proj(x2d, w, tm):
    n, d = x2d.shape
    d_ssm, d_conv, d_att = w['wz'].shape[1], w['wxbc'].shape[1], w['wq'].shape[1]
    row = lambda width: pl.BlockSpec((tm, width), lambda i: (i, 0))
    return pl.pallas_call(
        functools.partial(_in_proj_kernel, q_scale=w['q_scale']),
        grid=(n // tm,),
        in_specs=[row(d), _full(w['wz'].shape), _full(w['wxbc'].shape), _full(w['wdt'].shape),
                  _full(w['wq'].shape), _full(w['wk'].shape), _full(w['wv'].shape)],
        out_specs=[row(d_ssm), row(d_conv), row(LANES), row(d_att), row(d_att), row(d_att)],
        out_shape=[jax.ShapeDtypeStruct((n, d_ssm), F32), jax.ShapeDtypeStruct((n, d_conv), F32),
                   jax.ShapeDtypeStruct((n, LANES), F32), jax.ShapeDtypeStruct((n, d_att), BF16),
                   jax.ShapeDtypeStruct((n, d_att), BF16), jax.ShapeDtypeStruct((n, d_att), BF16)],
        compiler_params=_cparams(("parallel",)),
        name="in_proj",
    )(x2d, w['wz'], w['wxbc'], w['wdt'], w['wq'], w['wk'], w['wv'])


def _ssd_kernel(xbc_ref, dt_ref, cprev_ref, h0_ref, convw_ref, convb_ref, dtb_ref, alog_ref,
                dskip_ref, tril_ref, ex_ref, y_ref, hout_ref, cbuf, hst, xc_s, dt_s,
                *, blk, sub, d_ssm, d_state, head_dim):
    c = pl.program_id(1)
    gw = d_ssm // SSM_GROUPS

    @pl.when(c == 0)
    def _():
        hst[...] = h0_ref[0]
        cbuf[0:SUBLANES, :] = cprev_ref[0]

    _conv_silu(cbuf, xbc_ref[0], xc_s, convw_ref, convb_ref)

    dtr = dt_ref[0] + dtb_ref[...]
    dt_s[...] = jnp.maximum(dtr, 0.0) + jnp.log1p(jnp.exp(-jnp.abs(dtr)))
    a = -jnp.exp(alog_ref[...])

    row = lax.broadcasted_iota(jnp.int32, (sub, sub), 0)
    col = lax.broadcasted_iota(jnp.int32, (sub, sub), 1)
    causal = row >= col
    lane = lax.broadcasted_iota(jnp.int32, (sub, LANES), 1)
    first_head = lane < head_dim
    assert 2 * head_dim == LANES
    pairs_per_group = gw // LANES

    for sc in range(blk // sub):
        r0 = sc * sub
        dt = dt_s[r0:r0 + sub, :]
        acs = sum(jnp.dot(tril_ref[...], piece, preferred_element_type=F32) for piece in _split3(dt * a))
        acs_t = acs.T
        acs_p, dt_p = _split3(acs), _split3(dt)
        for g in range(SSM_GROUPS):
            gs = slice(g * gw, (g + 1) * gw)
            xs = xc_s[r0:r0 + sub, gs]
            acs_e = sum(jnp.dot(piece, ex_ref[:, gs], preferred_element_type=F32) for piece in acs_p)
            dt_e = sum(jnp.dot(piece, ex_ref[:, gs], preferred_element_type=F32) for piece in dt_p)
            last_e = acs_e[sub - 1:sub, :]
            dtx = dt_e * xs
            xw = (jnp.exp(last_e - acs_e) * dtx).astype(BF16)
            dtx = dtx.astype(BF16)
            eacs_e = jnp.exp(acs_e)
            b0 = d_ssm + g * d_state
            c0 = d_ssm + (SSM_GROUPS + g) * d_state
            bm = xc_s[r0:r0 + sub, b0:b0 + d_state].astype(BF16)
            cm = xc_s[r0:r0 + sub, c0:c0 + d_state].astype(BF16)
            hg = hst[g]
            cb = lax.dot_general(cm, bm, (((1,), (1,)), ((), ())), preferred_element_type=F32)
            y_off = jnp.dot(cm, hg.astype(BF16), preferred_element_type=F32)
            for jp in range(pairs_per_group):
                lg = slice(jp * LANES, (jp + 1) * LANES)
                p0 = g * gw + jp * LANES
                h0 = p0 // head_dim
                xpair = dtx[:, lg]
                outs = []
                for hh in (h0, h0 + 1):
                    seg = acs[:, hh:hh + 1] - acs_t[hh:hh + 1, :]
                    dec = jnp.exp(jnp.where(causal, seg, NEG_BIG))
                    outs.append(jnp.dot((cb * dec).astype(BF16), xpair, preferred_element_type=F32))
                y_diag = jnp.where(first_head, outs[0], outs[1])
                sl = slice(p0, p0 + LANES)
                y_ref[0, r0:r0 + sub, sl] = y_diag + y_off[:, lg] * eacs_e[:, lg] + dskip_ref[:, sl] * xs[:, lg]
            st = lax.dot_general(bm, xw, (((0,), (0,)), ((), ())), preferred_element_type=F32)
            hst[g] = hg * jnp.exp(last_e) + st

    @pl.when(c == pl.num_programs(1) - 1)
    def _():
        hout_ref[0] = hst[...]


def _ssd(xbc, dt, cprev8, h0t, p, blk):
    b, t, d_conv = xbc.shape
    d_ssm = p['dskip_e'].shape[1]
    d_state = (d_conv - d_ssm) // (2 * SSM_GROUPS)
    gw = d_ssm // SSM_GROUPS
    sub = min(blk, SSD_SUB)
    tril = jnp.tril(jnp.ones((sub, sub), BF16))
    kern = functools.partial(_ssd_kernel, blk=blk, sub=sub, d_ssm=d_ssm, d_state=d_state,
                             head_dim=d_ssm // p['a_log_heads'])
    return pl.pallas_call(
        kern,
        grid=(b, t // blk),
        in_specs=[pl.BlockSpec((1, blk, d_conv), lambda i, c: (i, c, 0)),
                  pl.BlockSpec((1, blk, LANES), lambda i, c: (i, c, 0)),
                  pl.BlockSpec((1, SUBLANES, d_conv), lambda i, c: (i, 0, 0)),
                  pl.BlockSpec((1, SSM_GROUPS, d_state, gw), lambda i, c: (i, 0, 0, 0)),
                  _full(p['conv_w'].shape), _full(p['conv_b'].shape), _full(p['dt_bias'].shape),
                  _full(p['a_log'].shape), _full(p['dskip_e'].shape), _full(tril.shape),
                  _full(p['expand'].shape)],
        out_specs=[pl.BlockSpec((1, blk, d_ssm), lambda i, c: (i, c, 0)),
                   pl.BlockSpec((1, SSM_GROUPS, d_state, gw), lambda i, c: (i, 0, 0, 0))],
        out_shape=[jax.ShapeDtypeStruct((b, t, d_ssm), F32),
                   jax.ShapeDtypeStruct((b, SSM_GROUPS, d_state, gw), F32)],
        scratch_shapes=[pltpu.VMEM((blk + SUBLANES, d_conv), F32),
                        pltpu.VMEM((SSM_GROUPS, d_state, gw), F32),
                        pltpu.VMEM((blk, d_conv), F32), pltpu.VMEM((blk, LANES), F32)],
        compiler_params=_cparams(("parallel", "arbitrary")),
        name="ssd",
    )(xbc, dt, cprev8, h0t, p['conv_w'], p['conv_b'], p['dt_bias'], p['a_log'], p['dskip_e'], tril,
      p['expand'])


def _attn_kernel(q_ref, kp_ref, kc_ref, vp_ref, vc_ref, bias_ref, o_ref, kwin, vwin,
                 *, past, qb, sub, kw, heads, head_dim, mask_positions):
    c = pl.program_id(1)
    kwin[0:past, :] = kp_ref[0]
    kwin[past:past + qb, :] = kc_ref[0]
    vwin[0:past, :] = vp_ref[0]
    vwin[past:past + qb, :] = vc_ref[0]
    lane = lax.broadcasted_iota(jnp.int32, (sub, LANES), 1)
    first_head = lane < head_dim

    def sub_block(i, masked):
        r0 = pl.multiple_of(i * sub, sub)
        if masked:
            col = lax.broadcasted_iota(jnp.int32, (2 * sub, kw), 1)
            valid = (col + (c * qb - past + i * sub)) >= 0
        for jp in range(heads * head_dim // LANES):
            ls = slice(jp * LANES, (jp + 1) * LANES)
            q2 = q_ref[0, pl.ds(r0, sub), ls]
            k2 = kwin[pl.ds(r0, kw), ls]
            v2 = vwin[pl.ds(r0, kw), ls]
            zero = jnp.zeros_like(q2)
            qq = jnp.concatenate([jnp.where(first_head, q2, zero), jnp.where(first_head, zero, q2)], axis=0)
            s = lax.dot_general(qq, k2, (((1,), (1,)), ((), ())), preferred_element_type=F32)
            s = s + bias_ref[jp]
            if masked:
                s = jnp.where(valid, s, NEG_BIG)
            m = jnp.max(s, axis=-1, keepdims=True)
            e = jnp.exp(s - m)
            l = jnp.sum(e, axis=-1, keepdims=True)
            o = jnp.dot(e.astype(BF16), v2, preferred_element_type=F32) / l
            o_ref[0, pl.ds(r0, sub), ls] = jnp.where(first_head, o[:sub], o[sub:])

    def run(masked):
        def body(i, carry):
            sub_block(i, masked)
            return carry
        lax.fori_loop(0, qb // sub, body, 0)

    if mask_positions:
        pl.when(c == 0)(lambda: run(True))
        pl.when(c > 0)(lambda: run(False))
    else:
        run(False)


def _attention(q, k, v, k_past, v_past, bias, *, qb, sub, mask_positions, heads):
    b, t, d = q.shape
    kw = bias.shape[2]
    bias = bias.reshape(heads // 2, 2 * sub, kw)
    if k_past is None:
        past = qb
        prev_spec = pl.BlockSpec((1, past, d), lambda i, c: (i, jnp.maximum(c - 1, 0), 0))
        k_past, v_past = k, v
    else:
        assert t == qb
        past = k_past.shape[1]
        prev_spec = pl.BlockSpec((1, past, d), lambda i, c: (i, 0, 0))
    cur_spec = pl.BlockSpec((1, qb, d), lambda i, c: (i, c, 0))
    kern = functools.partial(_attn_kernel, past=past, qb=qb, sub=sub, kw=kw, heads=heads,
                             head_dim=d // heads, mask_positions=mask_positions)
    return pl.pallas_call(
        kern,
        grid=(b, t // qb),
        in_specs=[cur_spec, prev_spec, cur_spec, prev_spec, cur_spec, _full(bias.shape)],
        out_specs=pl.BlockSpec((1, qb, d), lambda i, c: (i, c, 0)),
        out_shape=jax.ShapeDtypeStruct((b, t, d), F32),
        scratch_shapes=[pltpu.VMEM((past + qb, d), BF16), pltpu.VMEM((past + qb, d), BF16)],
        compiler_params=_cparams(("parallel", "arbitrary")),
        name="attention",
    )(q, k_past, k, v_past, v, bias)


def _band_bias(table, sub, kw, chunked):
    assert kw - sub == BAND_PAST or not chunked
    p_len = sub + kw
    dist = (kw - sub) + (sub - 1) - jnp.arange(p_len)
    vec = table[jnp.clip(dist, -REL_CLIP, REL_CLIP) + REL_CLIP].T.astype(F32)
    skew = jnp.tile(vec, (1, sub))[:, :sub * (p_len - 1)].reshape(-1, sub, p_len - 1)
    bias = skew[:, :, sub - 1:sub - 1 + kw]
    if chunked:
        s_loc = jnp.arange(kw)[None, :] - (jnp.arange(sub)[:, None] // CHUNK) * CHUNK
        bias = jnp.where(((s_loc >= 0) & (s_loc < BAND_PAST + CHUNK))[None], bias, NEG_BIG)
    return bias


def _layer_norm(r, g, b):
    mu = jnp.mean(r, axis=-1, keepdims=True)
    d = r - mu
    var = jnp.mean(d * d, axis=-1, keepdims=True)
    return d * lax.rsqrt(var + NORM_EPS) * g + b


def _post_mixer_kernel(y_ref, z_ref, o_ref, x_ref, wo_ref, gs_ref, ga_ref, lg_ref, lb_ref, rwt_ref, rb_ref,
                       hx_ref, cscr, *, alpha, n_experts, d_ssm):
    u = y_ref[...] * _silu(z_ref[...])
    ys = u * lax.rsqrt(jnp.mean(u * u, axis=-1, keepdims=True) + NORM_EPS) * gs_ref[...]
    o = o_ref[...]
    oa = o * lax.rsqrt(jnp.mean(o * o, axis=-1, keepdims=True) + NORM_EPS) * ga_ref[...]
    mix = (jnp.dot(ys.astype(BF16), wo_ref[0:d_ssm, :], preferred_element_type=F32)
           + jnp.dot(oa.astype(BF16), wo_ref[d_ssm:, :], preferred_element_type=F32))
    h = _layer_norm(alpha * x_ref[...] + mix, lg_ref[...], lb_ref[...])

    epg = EXPERTS_PER_GROUP
    assert n_experts == N_EXPERT_GROUPS * epg and epg == 4 and N_EXPERT_GROUPS <= SUBLANES
    h0 = h.astype(BF16)
    h1 = (h - h0.astype(F32)).astype(BF16)
    contract_last = (((1,), (1,)), ((), ()))
    lt0 = lax.dot_general(rwt_ref[...], h0, contract_last, preferred_element_type=F32)
    lt1 = lax.dot_general(rwt_ref[...], h1, contract_last, preferred_element_type=F32)
    rr = ROUTER_ROWS
    logits = lt0[0:rr] + lt0[rr:2 * rr] + lt0[2 * rr:3 * rr] + lt1[0:rr] + lt1[rr:2 * rr]
    grp = lax.broadcasted_iota(jnp.int32, (SUBLANES, logits.shape[1]), 0)
    real = grp < N_EXPERT_GROUPS
    lg = [jnp.where(real, logits[SUBLANES * j:SUBLANES * (j + 1)], -jnp.inf) for j in range(epg)]
    mx = jnp.max(functools.reduce(jnp.maximum, lg), axis=0, keepdims=True)
    ex = [jnp.exp(v - mx) for v in lg]
    zsum = jnp.sum(functools.reduce(lambda p, q: p + q, ex), axis=0, keepdims=True)
    score = [e / zsum for e in ex]
    sel = [score[j] + rb_ref[SUBLANES * j:SUBLANES * (j + 1), :] for j in range(epg)]

    def top2_sum(v):
        a, b = jnp.maximum(v[0], v[1]), jnp.minimum(v[0], v[1])
        c, d = jnp.maximum(v[2], v[3]), jnp.minimum(v[2], v[3])
        return jnp.maximum(a, c) + jnp.maximum(jnp.minimum(a, c), jnp.maximum(b, d))

    gscore = jnp.where(real, top2_sum(sel), -jnp.inf)
    gbest = jnp.max(gscore, axis=0, keepdims=True)
    gidx = jnp.min(jnp.where(gscore == gbest, grp, SUBLANES), axis=0, keepdims=True)
    chosen = grp == gidx
    in_sel = [jnp.sum(jnp.where(chosen, v, 0.0), axis=0, keepdims=True) for v in sel]
    in_score = [jnp.sum(jnp.where(chosen, v, 0.0), axis=0, keepdims=True) for v in score]

    def argmax_first(vals, exclude=None):
        bv, bi = None, None
        for j, v in enumerate(vals):
            if exclude is not None:
                v = jnp.where(exclude == j, -jnp.inf, v)
            if bv is None:
                bv, bi = v, jnp.zeros_like(gidx)
            else:
                upd = v > bv
                bv = jnp.where(upd, v, bv)
                bi = jnp.where(upd, j, bi)
        return bi

    j1 = argmax_first(in_sel)
    j2 = argmax_first(in_sel, exclude=j1)
    s1 = functools.reduce(lambda p, q: p + q, [jnp.where(j1 == j, in_score[j], 0.0) for j in range(epg)])
    s2 = functools.reduce(lambda p, q: p + q, [jnp.where(j2 == j, in_score[j], 0.0) for j in range(epg)])
    g1, g2 = s1 / (s1 + s2), s2 / (s1 + s2)
    first_lo = j1 < j2
    lo, hi = jnp.where(first_lo, j1, j2), jnp.where(first_lo, j2, j1)
    pair = jnp.where(lo == 0, hi - 1, jnp.where(lo == 1, hi + 1, PAIRS_PER_GROUP - 1))
    bucket = gidx * PAIRS_PER_GROUP + pair
    cscr[...] = jnp.zeros_like(cscr)
    cscr[ROUTE_BUCKET:ROUTE_BUCKET + 1, :] = bucket.astype(F32)
    cscr[ROUTE_GATE_LO:ROUTE_GATE_LO + 1, :] = jnp.where(first_lo, g1, g2)
    cscr[ROUTE_GATE_HI:ROUTE_GATE_HI + 1, :] = jnp.where(first_lo, g2, g1)
    d = h.shape[1]
    hx_ref[:, :d] = h
    hx_ref[:, d:] = cscr[...].T


def _post_mixer(y, z, o, x, p, tm, alpha):
    n, d = x.shape
    d_ssm = y.shape[1]
    row = lambda width: pl.BlockSpec((tm, width), lambda i: (i, 0))
    n_experts = p['n_experts']
    kern = functools.partial(_post_mixer_kernel, alpha=alpha, n_experts=n_experts, d_ssm=d_ssm)
    return pl.pallas_call(
        kern,
        grid=(n // tm,),
        in_specs=[row(d_ssm), row(d_ssm), row(o.shape[1]), row(d), _full(p['w_out'].shape),
                  _full(p['ssd_norm_g'].shape), _full(p['attn_norm_g'].shape), _full(p['ln1_g'].shape),
                  _full(p['ln1_b'].shape), _full(p['router_wt'].shape), _full(p['router_bias'].shape)],
        out_specs=row(d + LANES),
        out_shape=jax.ShapeDtypeStruct((n, d + LANES), F32),
        scratch_shapes=[pltpu.VMEM((LANES, tm), F32)],
        compiler_params=_cparams(("parallel",)),
        name="post_mixer",
    )(y, z, o, x, p['w_out'], p['ssd_norm_g'], p['attn_norm_g'], p['ln1_g'], p['ln1_b'], p['router_wt'],
      p['router_bias'])


def _start_row_gather(idx_ref, tile, src_hbm, buf, sem, slot, rows):
    for r in range(rows):
        row = idx_ref[tile, r]
        pltpu.make_async_copy(src_hbm.at[pl.ds(row, 1), :], buf.at[slot, pl.ds(r, 1), :], sem.at[slot]).start()


def _wait_row_gather(src_hbm, buf, sem, slot, rows):
    pltpu.make_async_copy(src_hbm.at[pl.ds(0, rows), :], buf.at[slot], sem.at[slot]).wait()


def _prefetched_rows(idx_ref, src_hbm, buf, sem, rows):
    i = pl.program_id(0)
    slot = lax.rem(i, 2)

    @pl.when(i == 0)
    def _():
        _start_row_gather(idx_ref, 0, src_hbm, buf, sem, 0, rows)

    @pl.when(i + 1 < pl.num_programs(0))
    def _():
        _start_row_gather(idx_ref, i + 1, src_hbm, buf, sem, 1 - slot, rows)

    _wait_row_gather(src_hbm, buf, sem, slot, rows)
    return slot


def _moe_kernel(src_ref, elo_ref, ehi_ref, nused_ref, hx_hbm, wg_lo, wu_lo, wd_lo, wg_hi, wu_hi, wd_hi,
                y_ref, xbuf, sem, *, tm, d):
    del elo_ref, ehi_ref
    slot = _prefetched_rows(src_ref, hx_hbm, xbuf, sem, tm)

    @pl.when(pl.program_id(0) < nused_ref[0])
    def _():
        x = xbuf[slot]
        xb = x[:, :d].astype(BF16)

        def ffn(wg, wu, wd):
            gate = jnp.dot(xb, wg[0], preferred_element_type=F32)
            up = jnp.dot(xb, wu[0], preferred_element_type=F32)
            return jnp.dot((_silu(gate) * up).astype(BF16), wd[0], preferred_element_type=F32)

        y_ref[...] = (x[:, d + ROUTE_GATE_LO:d + ROUTE_GATE_LO + 1] * ffn(wg_lo, wu_lo, wd_lo)
                      + x[:, d + ROUTE_GATE_HI:d + ROUTE_GATE_HI + 1] * ffn(wg_hi, wu_hi, wd_hi))

    @pl.when(pl.program_id(0) >= nused_ref[0])
    def _():
        y_ref[...] = jnp.zeros_like(y_ref)


def _moe(hx, src2d, elo, ehi, nused, p, d):
    n_tiles, tm = src2d.shape
    _, _, d_exp = p['w_gate'].shape
    wspec = lambda sel: pl.BlockSpec((1, d, d_exp), lambda i, s, lo, hi, nu: ((lo if sel == 0 else hi)[i], 0, 0))
    wdspec = lambda sel: pl.BlockSpec((1, d_exp, d), lambda i, s, lo, hi, nu: ((lo if sel == 0 else hi)[i], 0, 0))
    return pl.pallas_call(
        functools.partial(_moe_kernel, tm=tm, d=d),
        grid_spec=pltpu.PrefetchScalarGridSpec(
            num_scalar_prefetch=4,
            grid=(n_tiles,),
            in_specs=[pl.BlockSpec(memory_space=pl.ANY), wspec(0), wspec(0), wdspec(0),
                      wspec(1), wspec(1), wdspec(1)],
            out_specs=pl.BlockSpec((tm, d), lambda i, *_: (i, 0)),
            scratch_shapes=[pltpu.VMEM((2, tm, hx.shape[1]), F32), pltpu.SemaphoreType.DMA((2,))]),
        out_shape=jax.ShapeDtypeStruct((n_tiles * tm, d), F32),
        compiler_params=_cparams(("arbitrary",)),
        name="moe",
    )(src2d, elo, ehi, nused, hx, p['w_gate'], p['w_up'], p['w_down'], p['w_gate'], p['w_up'], p['w_down'])


def _finalize_kernel(pos_ref, y_hbm, hx_ref, lg_ref, lb_ref, out_ref, ybuf, sem, *, tm, d, alpha):
    slot = _prefetched_rows(pos_ref, y_hbm, ybuf, sem, tm)
    out_ref[...] = _layer_norm(alpha * hx_ref[:, :d] + ybuf[slot], lg_ref[...], lb_ref[...])


def _finalize(y_sorted, pos2d, hx, p, d, alpha):
    n_tiles, tm = pos2d.shape
    return pl.pallas_call(
        functools.partial(_finalize_kernel, tm=tm, d=d, alpha=alpha),
        grid_spec=pltpu.PrefetchScalarGridSpec(
            num_scalar_prefetch=1,
            grid=(n_tiles,),
            in_specs=[pl.BlockSpec(memory_space=pl.ANY),
                      pl.BlockSpec((tm, hx.shape[1]), lambda i, pos: (i, 0)),
                      pl.BlockSpec(p['ln2_g'].shape, lambda i, pos: (0, 0)),
                      pl.BlockSpec(p['ln2_b'].shape, lambda i, pos: (0, 0))],
            out_specs=pl.BlockSpec((tm, d), lambda i, pos: (i, 0)),
            scratch_shapes=[pltpu.VMEM((2, tm, d), F32), pltpu.SemaphoreType.DMA((2,))]),
        out_shape=jax.ShapeDtypeStruct((n_tiles * tm, d), F32),
        compiler_params=_cparams(("arbitrary",)),
        name="finalize",
    )(pos2d, y_sorted, hx, p['ln2_g'], p['ln2_b'])


def _route_tables(bucket, n_groups, tm):
    n = bucket.shape[0]
    n_buckets = n_groups * PAIRS_PER_GROUP
    n_tiles = n // tm + n_buckets
    onehot = bucket.reshape(n // tm, tm)[:, :, None] == jnp.arange(n_buckets, dtype=jnp.int32)
    tril = jnp.tril(jnp.ones((tm, tm), BF16))
    within = jnp.einsum('ij,tjb->tib', tril, onehot.astype(BF16), preferred_element_type=F32)
    block_counts = within[:, -1, :]
    before = jnp.cumsum(block_counts, axis=0) - block_counts
    counts = (before[-1] + block_counts[-1]).astype(jnp.int32)
    tiles_per = (counts + tm - 1) // tm
    tile_end = jnp.cumsum(tiles_per)
    start = ((tile_end - tiles_per) * tm).astype(F32)
    pos = jnp.sum(jnp.where(onehot, within - 1.0 + before[:, None, :] + start, 0.0), axis=-1)
    pos = pos.reshape(n).astype(jnp.int32)
    tile_ids = jnp.arange(n_tiles, dtype=jnp.int32)
    tile_bucket = jnp.sum(tile_end[None, :] <= tile_ids[:, None], axis=1)
    tile_bucket = jnp.minimum(tile_bucket, n_buckets - 1).astype(jnp.int32)
    order = jnp.argsort(bucket * n + jnp.arange(n, dtype=jnp.int32)).astype(jnp.int32)
    rank = (tile_ids - (tile_end - tiles_per)[tile_bucket])[:, None] * tm + jnp.arange(tm, dtype=jnp.int32)
    dense = (jnp.cumsum(counts) - counts)[tile_bucket][:, None] + rank
    src = jnp.where(rank < counts[tile_bucket][:, None], order[jnp.clip(dense, 0, n - 1)], 0).reshape(-1)
    group, pair = tile_bucket // PAIRS_PER_GROUP, tile_bucket % PAIRS_PER_GROUP
    elo = group * EXPERTS_PER_GROUP + jnp.asarray([lo for lo, _ in PAIRS], jnp.int32)[pair]
    ehi = group * EXPERTS_PER_GROUP + jnp.asarray([hi for _, hi in PAIRS], jnp.int32)[pair]
    return pos, src.reshape(n_tiles, tm), elo, ehi, tile_end[-1:].astype(jnp.int32)


def _pick(n, candidates):
    for c in candidates:
        if n % c == 0:
            return c
    raise ValueError(f"no tile size for {n}")


def _router_rows(per_expert):
    a = per_expert.reshape(N_EXPERT_GROUPS, EXPERTS_PER_GROUP, -1).swapaxes(0, 1)
    a = jnp.pad(a, ((0, 0), (0, SUBLANES - N_EXPERT_GROUPS), (0, 0)))
    return a.reshape(ROUTER_ROWS, -1)


def _layer_params(i, w_in, conv_w, conv_b, dt_bias, a_log, d_skip, ssd_norm_g, attn_norm_g, w_out,
                  ln1_g, ln1_b, router_w, router_bias, w_gate, w_up, w_down, ln2_g, ln2_b, dims):
    d_ssm, d_conv, heads, d_att, att_heads = dims
    o = 0
    wi = w_in[i]
    wz = wi[:, o:o + d_ssm]; o += d_ssm
    wxbc = wi[:, o:o + d_conv]; o += d_conv
    wdt = wi[:, o:o + heads]; o += heads
    wq = wi[:, o:o + d_att]; o += d_att
    wk = wi[:, o:o + d_att]; o += d_att
    wv = wi[:, o:o + d_att]
    pad_lanes = lambda v: jnp.pad(v, ((0, 0), (0, LANES - v.shape[1])))
    head_dim = d_ssm // heads
    expand = (jnp.arange(LANES)[:, None] == (jnp.arange(d_ssm)[None, :] // head_dim)).astype(BF16)
    return {
        'wz': wz.astype(BF16), 'wxbc': wxbc.astype(BF16), 'wdt': pad_lanes(wdt).astype(BF16),
        'wq': wq.astype(BF16), 'wk': wk.astype(BF16), 'wv': wv.astype(BF16),
        'q_scale': float(d_att // att_heads) ** -0.5,
        'conv_w': conv_w[i], 'conv_b': conv_b[i][None, :],
        'dt_bias': pad_lanes(dt_bias[i][None, :]), 'a_log': pad_lanes(a_log[i][None, :]),
        'dskip_e': jnp.repeat(d_skip[i], head_dim)[None, :], 'expand': expand, 'a_log_heads': heads,
        'ssd_norm_g': ssd_norm_g[i][None, :], 'attn_norm_g': attn_norm_g[i][None, :],
        'w_out': w_out[i].astype(BF16), 'ln1_g': ln1_g[i][None, :], 'ln1_b': ln1_b[i][None, :],
        'router_wt': jnp.pad(jnp.concatenate([_router_rows(piece) for piece in _split3(router_w.T)]),
                             ((0, LANES - 3 * ROUTER_ROWS), (0, 0))),
        'router_bias': _router_rows(router_bias[:, None]), 'n_experts': router_bias.shape[0],
        'w_gate': w_gate[i].astype(BF16), 'w_up': w_up[i].astype(BF16), 'w_down': w_down[i].astype(BF16),
        'ln2_g': ln2_g[i][None, :], 'ln2_b': ln2_b[i][None, :],
    }


def _trunk(x, layers, biases, conv_prev, h0, k_prev, v_prev, alpha, heads):
    b, t, d = x.shape
    n = b * t
    prompt = k_prev is None
    ks, vs, hs, cs = [], [], [], []
    x2 = x.reshape(n, d)
    tm = _pick(n, (512, 256, 128, 64, 32, 16, 8))
    tm_fin = _pick(n, (512, 256))
    n_buckets = N_EXPERT_GROUPS * PAIRS_PER_GROUP
    moe_tile = MOE_TILE if n >= 2 * n_buckets * MOE_TILE else MOE_TILE_SMALL
    assert n % moe_tile == 0
    for i, p in enumerate(layers):
        z, xbc, dt, q, k, v = _in_proj(x2, p, tm)
        d_conv = xbc.shape[1]
        d_ssm = z.shape[1]
        d_state = (d_conv - d_ssm) // (2 * SSM_GROUPS)
        xbc3 = xbc.reshape(b, t, d_conv)
        cprev8 = jnp.pad(conv_prev[i], ((0, 0), (SUBLANES - (CONV_W - 1), 0), (0, 0)))
        h0t = jnp.swapaxes(h0[i].reshape(b, SSM_GROUPS, d_ssm // SSM_GROUPS, d_state), 2, 3)
        blk = _pick(t, (512, 256, 128, 64, 32, 16, 8))
        y, ht = _ssd(xbc3, dt.reshape(b, t, LANES), cprev8, h0t, p, blk)
        q3, k3, v3 = (a.reshape(b, t, -1) for a in (q, k, v))
        if prompt:
            o = _attention(q3, k3, v3, None, None, biases[i], qb=BAND_PAST, sub=ATT_SUB,
                           mask_positions=True, heads=heads)
            keep = min(BAND_PAST, t)
            k_state, v_state = k3[:, t - keep:], v3[:, t - keep:]
        else:
            o = _attention(q3, k3, v3, k_prev[i], v_prev[i], biases[i], qb=t, sub=t,
                           mask_positions=False, heads=heads)
            k_state, v_state = k3, v3
        hx = _post_mixer(y.reshape(n, d_ssm), z, o.reshape(n, -1), x2, p, tm, alpha)
        bucket = hx[:, d + ROUTE_BUCKET].astype(jnp.int32)
        pos, src2d, elo, ehi, nused = _route_tables(bucket, N_EXPERT_GROUPS, moe_tile)
        y_sorted = _moe(hx, src2d, elo, ehi, nused, p, d)
        x2 = _finalize(y_sorted, pos.reshape(n // tm_fin, tm_fin), hx, p, d, alpha)
        head_dim = k3.shape[2] // heads
        ks.append(k_state.astype(F32).reshape(b, -1, heads, head_dim))
        vs.append(v_state.astype(F32).reshape(b, -1, heads, head_dim))
        hs.append(jnp.swapaxes(ht, 2, 3).reshape(h0[i].shape))
        full = jnp.concatenate([conv_prev[i], xbc3], axis=1) if t < CONV_W - 1 else xbc3
        cs.append(full[:, full.shape[1] - (CONV_W - 1):])
    return x2.reshape(b, t, d), jnp.stack(ks), jnp.stack(vs), jnp.stack(hs), jnp.stack(cs)


def kernel(x_prompt, x_sample, cache_k, cache_v, state_ssm, state_conv, w_in, conv_w, conv_b, dt_bias, a_log,
           d_skip, ssd_norm_g, attn_norm_g, rel_bias, w_out, ln1_g, ln1_b, router_w, router_bias, w_gate, w_up,
           w_down, ln2_g, ln2_b):
    depth = w_in.shape[0]
    heads = cache_k.shape[3]
    d_att = heads * cache_k.shape[4]
    d_conv = state_conv.shape[3]
    ssm_heads = state_ssm.shape[2]
    d_ssm = ssm_heads * state_ssm.shape[3]
    alpha = (2 * depth) ** 0.25
    dims = (d_ssm, d_conv, ssm_heads, d_att, heads)
    layers = [_layer_params(i, w_in, conv_w, conv_b, dt_bias, a_log, d_skip, ssd_norm_g, attn_norm_g, w_out,
                            ln1_g, ln1_b, router_w, router_bias, w_gate, w_up, w_down, ln2_g, ln2_b, dims)
              for i in range(depth)]
    bp, tp, _ = x_prompt.shape
    bs, ts, _ = x_sample.shape

    conv0 = jnp.zeros((depth, bp, CONV_W - 1, d_conv), F32)
    h00 = jnp.zeros((depth, bp) + state_ssm.shape[2:], F32)
    bias_p = [_band_bias(rel_bias[i], ATT_SUB, BAND_PAST + ATT_SUB, True) for i in range(depth)]
    y_p, k_p, v_p, h_p, c_p = _trunk(x_prompt, layers, bias_p, conv0, h00, None, None, alpha, heads)

    past = cache_k.shape[2]
    bias_s = [_band_bias(rel_bias[i], ts, past + ts, False) for i in range(depth)]
    ck = cache_k.reshape(depth, bs, past, d_att).astype(BF16)
    cv = cache_v.reshape(depth, bs, past, d_att).astype(BF16)
    y_s, k_s, v_s, h_s, c_s = _trunk(x_sample, layers, bias_s, state_conv, state_ssm, ck, cv, alpha, heads)
    return (y_p, y_s, k_p, v_p, h_p, c_p, k_s, v_s, h_s, c_s)
```

```python
import functools

import jax
import jax.numpy as jnp
from jax import lax
from jax.experimental import pallas as pl
from jax.experimental.pallas import tpu as pltpu

F32 = jnp.float32
BF16 = jnp.bfloat16

CHUNK = 64
BAND_PREV_CHUNKS = 8
BAND_PAST = BAND_PREV_CHUNKS * CHUNK
REL_CLIP = 256
SSM_GROUPS = 2
CONV_W = 4
N_EXPERT_GROUPS = 4
EXPERTS_PER_GROUP = 4
PAIRS = [(lo, hi) for lo in range(EXPERTS_PER_GROUP) for hi in range(lo + 1, EXPERTS_PER_GROUP)]
PAIRS_PER_GROUP = len(PAIRS)
ROUTE_BUCKET, ROUTE_GATE_LO, ROUTE_GATE_HI = 0, 1, 2
MOE_TILE = 256
MOE_TILE_SMALL = 64
ATT_SUB = 4 * CHUNK
SSD_SUB = 128
NORM_EPS = 1e-5
NEG_BIG = -1e30

LANES = 128
SUBLANES = 8
VMEM_LIMIT = 56 * 1024 * 1024
ROUTER_ROWS = SUBLANES * EXPERTS_PER_GROUP


def _cparams(sem):
    return pltpu.CompilerParams(dimension_semantics=sem, vmem_limit_bytes=VMEM_LIMIT)


def _silu(v):
    half = 0.5 * v
    return half + half * jnp.tanh(half)


def _split3(v):
    p1 = v.astype(BF16)
    r1 = v - p1.astype(F32)
    p2 = r1.astype(BF16)
    p3 = (r1 - p2.astype(F32)).astype(BF16)
    return p1, p2, p3


def _full(shape):
    return pl.BlockSpec(shape, lambda *_: (0,) * len(shape))


def _conv_silu(cbuf, new_rows, out_ref, convw_ref, convb_ref):
    n, width = new_rows.shape
    cbuf[SUBLANES:SUBLANES + n, :] = new_rows
    for c0 in range(0, width, LANES):
        cols = slice(c0, c0 + LANES)
        full = cbuf[:, cols]
        acc = jnp.broadcast_to(convb_ref[:, cols], (n, LANES))
        for i in range(CONV_W):
            back = CONV_W - 1 - i
            shifted = full if back == 0 else pltpu.roll(full, back, axis=0)
            acc = acc + shifted[SUBLANES:SUBLANES + n, :] * convw_ref[i:i + 1, cols]
        out_ref[:, cols] = _silu(acc)
    cbuf[0:SUBLANES, :] = cbuf[n:n + SUBLANES, :]


def _in_proj_kernel(x_ref, wz_ref, wxbc_ref, wdt_ref, wq_ref, wk_ref, wv_ref,
                    z_ref, xbc_ref, dt_ref, q_ref, k_ref, v_ref, *, q_scale):
    xb = x_ref[...].astype(BF16)
    z_ref[...] = jnp.dot(xb, wz_ref[...], preferred_element_type=F32)
    xbc_ref[...] = jnp.dot(xb, wxbc_ref[...], preferred_element_type=F32)
    dt_ref[...] = jnp.dot(xb, wdt_ref[...], preferred_element_type=F32)
    q_ref[...] = (jnp.dot(xb, wq_ref[...], preferred_element_type=F32) * q_scale).astype(BF16)
    k_ref[...] = jnp.dot(xb, wk_ref[...], preferred_element_type=F32).astype(BF16)
    v_ref[...] = jnp.dot(xb, wv_ref[...], preferred_element_type=F32).astype(BF16)


def _in_proj(x2d, w, tm):
    n, d = x2d.shape
    d_ssm, d_conv, d_att = w['wz'].shape[1], w['wxbc'].shape[1], w['wq'].shape[1]
    row = lambda width: pl.BlockSpec((tm, width), lambda i: (i, 0))
    return pl.pallas_call(
        functools.partial(_in_proj_kernel, q_scale=w['q_scale']),
        grid=(n // tm,),
        in_specs=[row(d), _full(w['wz'].shape), _full(w['wxbc'].shape), _full(w['wdt'].shape),
                  _full(w['wq'].shape), _full(w['wk'].shape), _full(w['wv'].shape)],
        out_specs=[row(d_ssm), row(d_conv), row(LANES), row(d_att), row(d_att), row(d_att)],
        out_shape=[jax.ShapeDtypeStruct((n, d_ssm), F32), jax.ShapeDtypeStruct((n, d_conv), F32),
                   jax.ShapeDtypeStruct((n, LANES), F32), jax.ShapeDtypeStruct((n, d_att), BF16),
                   jax.ShapeDtypeStruct((n, d_att), BF16), jax.ShapeDtypeStruct((n, d_att), BF16)],
        compiler_params=_cparams(("parallel",)),
        name="in_proj",
    )(x2d, w['wz'], w['wxbc'], w['wdt'], w['wq'], w['wk'], w['wv'])


def _ssd_kernel(xbc_ref, dt_ref, cprev_ref, h0_ref, convw_ref, convb_ref, dtb_ref, alog_ref,
                dskip_ref, tril_ref, ex_ref, y_ref, hout_ref, cbuf, hst, xc_s, dt_s,
                *, blk, sub, d_ssm, d_state, head_dim):
    c = pl.program_id(1)
    gw = d_ssm // SSM_GROUPS

    @pl.when(c == 0)
    def _():
        hst[...] = h0_ref[0]
        cbuf[0:SUBLANES, :] = cprev_ref[0]

    _conv_silu(cbuf, xbc_ref[0], xc_s, convw_ref, convb_ref)

    dtr = dt_ref[0] + dtb_ref[...]
    dt_s[...] = jnp.maximum(dtr, 0.0) + jnp.log1p(jnp.exp(-jnp.abs(dtr)))
    a = -jnp.exp(alog_ref[...])

    row = lax.broadcasted_iota(jnp.int32, (sub, sub), 0)
    col = lax.broadcasted_iota(jnp.int32, (sub, sub), 1)
    causal = row >= col
    lane = lax.broadcasted_iota(jnp.int32, (sub, LANES), 1)
    first_head = lane < head_dim
    assert 2 * head_dim == LANES
    pairs_per_group = gw // LANES

    for sc in range(blk // sub):
        r0 = sc * sub
        dt = dt_s[r0:r0 + sub, :]
        acs = sum(jnp.dot(tril_ref[...], piece, preferred_element_type=F32) for piece in _split3(dt * a))
        acs_t = acs.T
        acs_p, dt_p = _split3(acs), _split3(dt)
        for g in range(SSM_GROUPS):
            gs = slice(g * gw, (g + 1) * gw)
            xs = xc_s[r0:r0 + sub, gs]
            acs_e = sum(jnp.dot(piece, ex_ref[:, gs], preferred_element_type=F32) for piece in acs_p)
            dt_e = sum(jnp.dot(piece, ex_ref[:, gs], preferred_element_type=F32) for piece in dt_p)
            last_e = acs_e[sub - 1:sub, :]
            dtx = dt_e * xs
            xw = (jnp.exp(last_e - acs_e) * dtx).astype(BF16)
            dtx = dtx.astype(BF16)
            eacs_e = jnp.exp(acs_e)
            b0 = d_ssm + g * d_state
            c0 = d_ssm + (SSM_GROUPS + g) * d_state
            bm = xc_s[r0:r0 + sub, b0:b0 + d_state].astype(BF16)
            cm = xc_s[r0:r0 + sub, c0:c0 + d_state].astype(BF16)
            hg = hst[g]
            cb = lax.dot_general(cm, bm, (((1,), (1,)), ((), ())), preferred_element_type=F32)
            y_off = jnp.dot(cm, hg.astype(BF16), preferred_element_type=F32)
            for jp in range(pairs_per_group):
                lg = slice(jp * LANES, (jp + 1) * LANES)
                p0 = g * gw + jp * LANES
                h0 = p0 // head_dim
                xpair = dtx[:, lg]
                outs = []
                for hh in (h0, h0 + 1):
                    seg = acs[:, hh:hh + 1] - acs_t[hh:hh + 1, :]
                    dec = jnp.exp(jnp.where(causal, seg, NEG_BIG))
                    outs.append(jnp.dot((cb * dec).astype(BF16), xpair, preferred_element_type=F32))
                y_diag = jnp.where(first_head, outs[0], outs[1])
                sl = slice(p0, p0 + LANES)
                y_ref[0, r0:r0 + sub, sl] = y_diag + y_off[:, lg] * eacs_e[:, lg] + dskip_ref[:, sl] * xs[:, lg]
            st = lax.dot_general(bm, xw, (((0,), (0,)), ((), ())), preferred_element_type=F32)
            hst[g] = hg * jnp.exp(last_e) + st

    @pl.when(c == pl.num_programs(1) - 1)
    def _():
        hout_ref[0] = hst[...]


def _ssd(xbc, dt, cprev8, h0t, p, blk):
    b, t, d_conv = xbc.shape
    d_ssm = p['dskip_e'].shape[1]
    d_state = (d_conv - d_ssm) // (2 * SSM_GROUPS)
    gw = d_ssm // SSM_GROUPS
    sub = min(blk, SSD_SUB)
    tril = jnp.tril(jnp.ones((sub, sub), BF16))
    kern = functools.partial(_ssd_kernel, blk=blk, sub=sub, d_ssm=d_ssm, d_state=d_state,
                             head_dim=d_ssm // p['a_log_heads'])
    return pl.pallas_call(
        kern,
        grid=(b, t // blk),
        in_specs=[pl.BlockSpec((1, blk, d_conv), lambda i, c: (i, c, 0)),
                  pl.BlockSpec((1, blk, LANES), lambda i, c: (i, c, 0)),
                  pl.BlockSpec((1, SUBLANES, d_conv), lambda i, c: (i, 0, 0)),
                  pl.BlockSpec((1, SSM_GROUPS, d_state, gw), lambda i, c: (i, 0, 0, 0)),
                  _full(p['conv_w'].shape), _full(p['conv_b'].shape), _full(p['dt_bias'].shape),
                  _full(p['a_log'].shape), _full(p['dskip_e'].shape), _full(tril.shape),
                  _full(p['expand'].shape)],
        out_specs=[pl.BlockSpec((1, blk, d_ssm), lambda i, c: (i, c, 0)),
                   pl.BlockSpec((1, SSM_GROUPS, d_state, gw), lambda i, c: (i, 0, 0, 0))],
        out_shape=[jax.ShapeDtypeStruct((b, t, d_ssm), F32),
                   jax.ShapeDtypeStruct((b, SSM_GROUPS, d_state, gw), F32)],
        scratch_shapes=[pltpu.VMEM((blk + SUBLANES, d_conv), F32),
                        pltpu.VMEM((SSM_GROUPS, d_state, gw), F32),
                        pltpu.VMEM((blk, d_conv), F32), pltpu.VMEM((blk, LANES), F32)],
        compiler_params=_cparams(("parallel", "arbitrary")),
        name="ssd",
    )(xbc, dt, cprev8, h0t, p['conv_w'], p['conv_b'], p['dt_bias'], p['a_log'], p['dskip_e'], tril,
      p['expand'])


def _attn_kernel(q_ref, kp_ref, kc_ref, vp_ref, vc_ref, bias_ref, o_ref, kwin, vwin,
                 *, past, qb, sub, kw, heads, head_dim, mask_positions):
    c = pl.program_id(1)
    kwin[0:past, :] = kp_ref[0]
    kwin[past:past + qb, :] = kc_ref[0]
    vwin[0:past, :] = vp_ref[0]
    vwin[past:past + qb, :] = vc_ref[0]
    lane = lax.broadcasted_iota(jnp.int32, (sub, LANES), 1)
    first_head = lane < head_dim

    def sub_block(i, masked):
        r0 = pl.multiple_of(i * sub, sub)
        if masked:
            col = lax.broadcasted_iota(jnp.int32, (2 * sub, kw), 1)
            valid = (col + (c * qb - past + i * sub)) >= 0
        for jp in range(heads * head_dim // LANES):
            ls = slice(jp * LANES, (jp + 1) * LANES)
            q2 = q_ref[0, pl.ds(r0, sub), ls]
            k2 = kwin[pl.ds(r0, kw), ls]
            v2 = vwin[pl.ds(r0, kw), ls]
            zero = jnp.zeros_like(q2)
            qq = jnp.concatenate([jnp.where(first_head, q2, zero), jnp.where(first_head, zero, q2)], axis=0)
            s = lax.dot_general(qq, k2, (((1,), (1,)), ((), ())), preferred_element_type=F32)
            s = s + bias_ref[jp]
            if masked:
                s = jnp.where(valid, s, NEG_BIG)
            m = jnp.max(s, axis=-1, keepdims=True)
            e = jnp.exp(s - m)
            l = jnp.sum(e, axis=-1, keepdims=True)
            o = jnp.dot(e.astype(BF16), v2, preferred_element_type=F32) / l
            o_ref[0, pl.ds(r0, sub), ls] = jnp.where(first_head, o[:sub], o[sub:])

    def run(masked):
        def body(i, carry):
            sub_block(i, masked)
            return carry
        lax.fori_loop(0, qb // sub, body, 0)

    if mask_positions:
        pl.when(c == 0)(lambda: run(True))
        pl.when(c > 0)(lambda: run(False))
    else:
        run(False)


def _attention(q, k, v, k_past, v_past, bias, *, qb, sub, mask_positions, heads):
    b, t, d = q.shape
    kw = bias.shape[2]
    bias = bias.reshape(heads // 2, 2 * sub, kw)
    if k_past is None:
        past = qb
        prev_spec = pl.BlockSpec((1, past, d), lambda i, c: (i, jnp.maximum(c - 1, 0), 0))
        k_past, v_past = k, v
    else:
        assert t == qb
        past = k_past.shape[1]
        prev_spec = pl.BlockSpec((1, past, d), lambda i, c: (i, 0, 0))
    cur_spec = pl.BlockSpec((1, qb, d), lambda i, c: (i, c, 0))
    kern = functools.partial(_attn_kernel, past=past, qb=qb, sub=sub, kw=kw, heads=heads,
                             head_dim=d // heads, mask_positions=mask_positions)
    return pl.pallas_call(
        kern,
        grid=(b, t // qb),
        in_specs=[cur_spec, prev_spec, cur_spec, prev_spec, cur_spec, _full(bias.shape)],
        out_specs=pl.BlockSpec((1, qb, d), lambda i, c: (i, c, 0)),
        out_shape=jax.ShapeDtypeStruct((b, t, d), F32),
        scratch_shapes=[pltpu.VMEM((past + qb, d), BF16), pltpu.VMEM((past + qb, d), BF16)],
        compiler_params=_cparams(("parallel", "arbitrary")),
        name="attention",
    )(q, k_past, k, v_past, v, bias)


def _band_bias(table, sub, kw, chunked):
    assert kw - sub == BAND_PAST or not chunked
    p_len = sub + kw
    dist = (kw - sub) + (sub - 1) - jnp.arange(p_len)
    vec = table[jnp.clip(dist, -REL_CLIP, REL_CLIP) + REL_CLIP].T.astype(F32)
    skew = jnp.tile(vec, (1, sub))[:, :sub * (p_len - 1)].reshape(-1, sub, p_len - 1)
    bias = skew[:, :, sub - 1:sub - 1 + kw]
    if chunked:
        s_loc = jnp.arange(kw)[None, :] - (jnp.arange(sub)[:, None] // CHUNK) * CHUNK
        bias = jnp.where(((s_loc >= 0) & (s_loc < BAND_PAST + CHUNK))[None], bias, NEG_BIG)
    return bias


def _layer_norm(r, g, b):
    mu = jnp.mean(r, axis=-1, keepdims=True)
    d = r - mu
    var = jnp.mean(d * d, axis=-1, keepdims=True)
    return d * lax.rsqrt(var + NORM_EPS) * g + b


def _post_mixer_kernel(y_ref, z_ref, o_ref, x_ref, wo_ref, gs_ref, ga_ref, lg_ref, lb_ref, rwt_ref, rb_ref,
                       hx_ref, cscr, *, alpha, n_experts, d_ssm):
    u = y_ref[...] * _silu(z_ref[...])
    ys = u * lax.rsqrt(jnp.mean(u * u, axis=-1, keepdims=True) + NORM_EPS) * gs_ref[...]
    o = o_ref[...]
    oa = o * lax.rsqrt(jnp.mean(o * o, axis=-1, keepdims=True) + NORM_EPS) * ga_ref[...]
    mix = (jnp.dot(ys.astype(BF16), wo_ref[0:d_ssm, :], preferred_element_type=F32)
           + jnp.dot(oa.astype(BF16), wo_ref[d_ssm:, :], preferred_element_type=F32))
    h = _layer_norm(alpha * x_ref[...] + mix, lg_ref[...], lb_ref[...])

    epg = EXPERTS_PER_GROUP
    assert n_experts == N_EXPERT_GROUPS * epg and epg == 4 and N_EXPERT_GROUPS <= SUBLANES
    h0 = h.astype(BF16)
    h1 = (h - h0.astype(F32)).astype(BF16)
    contract_last = (((1,), (1,)), ((), ()))
    lt0 = lax.dot_general(rwt_ref[...], h0, contract_last, preferred_element_type=F32)
    lt1 = lax.dot_general(rwt_ref[...], h1, contract_last, preferred_element_type=F32)
    rr = ROUTER_ROWS
    logits = lt0[0:rr] + lt0[rr:2 * rr] + lt0[2 * rr:3 * rr] + lt1[0:rr] + lt1[rr:2 * rr]
    grp = lax.broadcasted_iota(jnp.int32, (SUBLANES, logits.shape[1]), 0)
    real = grp < N_EXPERT_GROUPS
    lg = [jnp.where(real, logits[SUBLANES * j:SUBLANES * (j + 1)], -jnp.inf) for j in range(epg)]
    mx = jnp.max(functools.reduce(jnp.maximum, lg), axis=0, keepdims=True)
    ex = [jnp.exp(v - mx) for v in lg]
    zsum = jnp.sum(functools.reduce(lambda p, q: p + q, ex), axis=0, keepdims=True)
    score = [e / zsum for e in ex]
    sel = [score[j] + rb_ref[SUBLANES * j:SUBLANES * (j + 1), :] for j in range(epg)]

    def top2_sum(v):
        a, b = jnp.maximum(v[0], v[1]), jnp.minimum(v[0], v[1])
        c, d = jnp.maximum(v[2], v[3]), jnp.minimum(v[2], v[3])
        return jnp.maximum(a, c) + jnp.maximum(jnp.minimum(a, c), jnp.maximum(b, d))

    gscore = jnp.where(real, top2_sum(sel), -jnp.inf)
    gbest = jnp.max(gscore, axis=0, keepdims=True)
    gidx = jnp.min(jnp.where(gscore == gbest, grp, SUBLANES), axis=0, keepdims=True)
    chosen = grp == gidx
    in_sel = [jnp.sum(jnp.where(chosen, v, 0.0), axis=0, keepdims=True) for v in sel]
    in_score = [jnp.sum(jnp.where(chosen, v, 0.0), axis=0, keepdims=True) for v in score]

    def argmax_first(vals, exclude=None):
        bv, bi = None, None
        for j, v in enumerate(vals):
            if exclude is not None:
                v = jnp.where(exclude == j, -jnp.inf, v)
            if bv is None:
                bv, bi = v, jnp.zeros_like(gidx)
            else:
                upd = v > bv
                bv = jnp.where(upd, v, bv)
                bi = jnp.where(upd, j, bi)
        return bi

    j1 = argmax_first(in_sel)
    j2 = argmax_first(in_sel, exclude=j1)
    s1 = functools.reduce(lambda p, q: p + q, [jnp.where(j1 == j, in_score[j], 0.0) for j in range(epg)])
    s2 = functools.reduce(lambda p, q: p + q, [jnp.where(j2 == j, in_score[j], 0.0) for j in range(epg)])
    g1, g2 = s1 / (s1 + s2), s2 / (s1 + s2)
    first_lo = j1 < j2
    lo, hi = jnp.where(first_lo, j1, j2), jnp.where(first_lo, j2, j1)
    pair = jnp.where(lo == 0, hi - 1, jnp.where(lo == 1, hi + 1, PAIRS_PER_GROUP - 1))
    bucket = gidx * PAIRS_PER_GROUP + pair
    cscr[...] = jnp.zeros_like(cscr)
    cscr[ROUTE_BUCKET:ROUTE_BUCKET + 1, :] = bucket.astype(F32)
    cscr[ROUTE_GATE_LO:ROUTE_GATE_LO + 1, :] = jnp.where(first_lo, g1, g2)
    cscr[ROUTE_GATE_HI:ROUTE_GATE_HI + 1, :] = jnp.where(first_lo, g2, g1)
    d = h.shape[1]
    hx_ref[:, :d] = h
    hx_ref[:, d:] = cscr[...].T


def _post_mixer(y, z, o, x, p, tm, alpha):
    n, d = x.shape
    d_ssm = y.shape[1]
    row = lambda width: pl.BlockSpec((tm, width), lambda i: (i, 0))
    n_experts = p['n_experts']
    kern = functools.partial(_post_mixer_kernel, alpha=alpha, n_experts=n_experts, d_ssm=d_ssm)
    return pl.pallas_call(
        kern,
        grid=(n // tm,),
        in_specs=[row(d_ssm), row(d_ssm), row(o.shape[1]), row(d), _full(p['w_out'].shape),
                  _full(p['ssd_norm_g'].shape), _full(p['attn_norm_g'].shape), _full(p['ln1_g'].shape),
                  _full(p['ln1_b'].shape), _full(p['router_wt'].shape), _full(p['router_bias'].shape)],
        out_specs=row(d + LANES),
        out_shape=jax.ShapeDtypeStruct((n, d + LANES), F32),
        scratch_shapes=[pltpu.VMEM((LANES, tm), F32)],
        compiler_params=_cparams(("parallel",)),
        name="post_mixer",
    )(y, z, o, x, p['w_out'], p['ssd_norm_g'], p['attn_norm_g'], p['ln1_g'], p['ln1_b'], p['router_wt'],
      p['router_bias'])


def _start_row_gather(idx_ref, tile, src_hbm, buf, sem, slot, rows):
    for r in range(rows):
        row = idx_ref[tile, r]
        pltpu.make_async_copy(src_hbm.at[pl.ds(row, 1), :], buf.at[slot, pl.ds(r, 1), :],
                              sem.at[slot]).start(priority=r % 2)


def _wait_row_gather(src_hbm, buf, sem, slot, rows):
    pltpu.make_async_copy(src_hbm.at[pl.ds(0, rows), :], buf.at[slot], sem.at[slot]).wait()


def _prefetched_rows(idx_ref, src_hbm, buf, sem, rows):
    i = pl.program_id(0)
    slot = lax.rem(i, 2)

    @pl.when(i == 0)
    def _():
        _start_row_gather(idx_ref, 0, src_hbm, buf, sem, 0, rows)

    @pl.when(i + 1 < pl.num_programs(0))
    def _():
        _start_row_gather(idx_ref, i + 1, src_hbm, buf, sem, 1 - slot, rows)

    _wait_row_gather(src_hbm, buf, sem, slot, rows)
    return slot


def _moe_kernel(src_ref, elo_ref, ehi_ref, nused_ref, hx_hbm, wg_lo, wu_lo, wd_lo, wg_hi, wu_hi, wd_hi,
                y_ref, xbuf, sem, *, tm, d):
    del elo_ref, ehi_ref
    slot = _prefetched_rows(src_ref, hx_hbm, xbuf, sem, tm)

    @pl.when(pl.program_id(0) < nused_ref[0])
    def _():
        x = xbuf[slot]
        xb = x[:, :d].astype(BF16)

        def ffn(wg, wu, wd):
            gate = jnp.dot(xb, wg[0], preferred_element_type=F32)
            up = jnp.dot(xb, wu[0], preferred_element_type=F32)
            return jnp.dot((_silu(gate) * up).astype(BF16), wd[0], preferred_element_type=F32)

        y_ref[...] = (x[:, d + ROUTE_GATE_LO:d + ROUTE_GATE_LO + 1] * ffn(wg_lo, wu_lo, wd_lo)
                      + x[:, d + ROUTE_GATE_HI:d + ROUTE_GATE_HI + 1] * ffn(wg_hi, wu_hi, wd_hi))

    @pl.when(pl.program_id(0) >= nused_ref[0])
    def _():
        y_ref[...] = jnp.zeros_like(y_ref)


def _moe(hx, src2d, elo, ehi, nused, p, d):
    n_tiles, tm = src2d.shape
    _, _, d_exp = p['w_gate'].shape
    wspec = lambda sel: pl.BlockSpec((1, d, d_exp), lambda i, s, lo, hi, nu: ((lo if sel == 0 else hi)[i], 0, 0))
    wdspec = lambda sel: pl.BlockSpec((1, d_exp, d), lambda i, s, lo, hi, nu: ((lo if sel == 0 else hi)[i], 0, 0))
    return pl.pallas_call(
        functools.partial(_moe_kernel, tm=tm, d=d),
        grid_spec=pltpu.PrefetchScalarGridSpec(
            num_scalar_prefetch=4,
            grid=(n_tiles,),
            in_specs=[pl.BlockSpec(memory_space=pl.ANY), wspec(0), wspec(0), wdspec(0),
                      wspec(1), wspec(1), wdspec(1)],
            out_specs=pl.BlockSpec((tm, d), lambda i, *_: (i, 0)),
            scratch_shapes=[pltpu.VMEM((2, tm, hx.shape[1]), F32), pltpu.SemaphoreType.DMA((2,))]),
        out_shape=jax.ShapeDtypeStruct((n_tiles * tm, d), F32),
        compiler_params=_cparams(("arbitrary",)),
        name="moe",
    )(src2d, elo, ehi, nused, hx, p['w_gate'], p['w_up'], p['w_down'], p['w_gate'], p['w_up'], p['w_down'])


def _finalize_kernel(pos_ref, y_hbm, hx_ref, lg_ref, lb_ref, out_ref, ybuf, sem, *, tm, d, alpha):
    slot = _prefetched_rows(pos_ref, y_hbm, ybuf, sem, tm)
    out_ref[...] = _layer_norm(alpha * hx_ref[:, :d] + ybuf[slot], lg_ref[...], lb_ref[...])


def _finalize(y_sorted, pos2d, hx, p, d, alpha):
    n_tiles, tm = pos2d.shape
    return pl.pallas_call(
        functools.partial(_finalize_kernel, tm=tm, d=d, alpha=alpha),
        grid_spec=pltpu.PrefetchScalarGridSpec(
            num_scalar_prefetch=1,
            grid=(n_tiles,),
            in_specs=[pl.BlockSpec(memory_space=pl.ANY),
                      pl.BlockSpec((tm, hx.shape[1]), lambda i, pos: (i, 0)),
                      pl.BlockSpec(p['ln2_g'].shape, lambda i, pos: (0, 0)),
                      pl.BlockSpec(p['ln2_b'].shape, lambda i, pos: (0, 0))],
            out_specs=pl.BlockSpec((tm, d), lambda i, pos: (i, 0)),
            scratch_shapes=[pltpu.VMEM((2, tm, d), F32), pltpu.SemaphoreType.DMA((2,))]),
        out_shape=jax.ShapeDtypeStruct((n_tiles * tm, d), F32),
        compiler_params=_cparams(("arbitrary",)),
        name="finalize",
    )(pos2d, y_sorted, hx, p['ln2_g'], p['ln2_b'])


def _route_tables(bucket, n_groups, tm):
    n = bucket.shape[0]
    n_buckets = n_groups * PAIRS_PER_GROUP
    n_tiles = n // tm + n_buckets
    onehot = bucket.reshape(n // tm, tm)[:, :, None] == jnp.arange(n_buckets, dtype=jnp.int32)
    tril = jnp.tril(jnp.ones((tm, tm), BF16))
    within = jnp.einsum('ij,tjb->tib', tril, onehot.astype(BF16), preferred_element_type=F32)
    block_counts = within[:, -1, :]
    before = jnp.cumsum(block_counts, axis=0) - block_counts
    counts = (before[-1] + block_counts[-1]).astype(jnp.int32)
    tiles_per = (counts + tm - 1) // tm
    tile_end = jnp.cumsum(tiles_per)
    start = ((tile_end - tiles_per) * tm).astype(F32)
    pos = jnp.sum(jnp.where(onehot, within - 1.0 + before[:, None, :] + start, 0.0), axis=-1)
    pos = pos.reshape(n).astype(jnp.int32)
    tile_ids = jnp.arange(n_tiles, dtype=jnp.int32)
    tile_bucket = jnp.sum(tile_end[None, :] <= tile_ids[:, None], axis=1)
    tile_bucket = jnp.minimum(tile_bucket, n_buckets - 1).astype(jnp.int32)
    order = jnp.argsort(bucket * n + jnp.arange(n, dtype=jnp.int32)).astype(jnp.int32)
    rank = (tile_ids - (tile_end - tiles_per)[tile_bucket])[:, None] * tm + jnp.arange(tm, dtype=jnp.int32)
    dense = (jnp.cumsum(counts) - counts)[tile_bucket][:, None] + rank
    src = jnp.where(rank < counts[tile_bucket][:, None], order[jnp.clip(dense, 0, n - 1)], 0).reshape(-1)
    group, pair = tile_bucket // PAIRS_PER_GROUP, tile_bucket % PAIRS_PER_GROUP
    elo = group * EXPERTS_PER_GROUP + jnp.asarray([lo for lo, _ in PAIRS], jnp.int32)[pair]
    ehi = group * EXPERTS_PER_GROUP + jnp.asarray([hi for _, hi in PAIRS], jnp.int32)[pair]
    return pos, src.reshape(n_tiles, tm), elo, ehi, tile_end[-1:].astype(jnp.int32)


def _pick(n, candidates):
    for c in candidates:
        if n % c == 0:
            return c
    raise ValueError(f"no tile size for {n}")


def _router_rows(per_expert):
    a = per_expert.reshape(N_EXPERT_GROUPS, EXPERTS_PER_GROUP, -1).swapaxes(0, 1)
    a = jnp.pad(a, ((0, 0), (0, SUBLANES - N_EXPERT_GROUPS), (0, 0)))
    return a.reshape(ROUTER_ROWS, -1)


def _layer_params(i, w_in, conv_w, conv_b, dt_bias, a_log, d_skip, ssd_norm_g, attn_norm_g, w_out,
                  ln1_g, ln1_b, router_w, router_bias, w_gate, w_up, w_down, ln2_g, ln2_b, dims):
    d_ssm, d_conv, heads, d_att, att_heads = dims
    o = 0
    wi = w_in[i]
    wz = wi[:, o:o + d_ssm]; o += d_ssm
    wxbc = wi[:, o:o + d_conv]; o += d_conv
    wdt = wi[:, o:o + heads]; o += heads
    wq = wi[:, o:o + d_att]; o += d_att
    wk = wi[:, o:o + d_att]; o += d_att
    wv = wi[:, o:o + d_att]
    pad_lanes = lambda v: jnp.pad(v, ((0, 0), (0, LANES - v.shape[1])))
    head_dim = d_ssm // heads
    expand = (jnp.arange(LANES)[:, None] == (jnp.arange(d_ssm)[None, :] // head_dim)).astype(BF16)
    return {
        'wz': wz.astype(BF16), 'wxbc': wxbc.astype(BF16), 'wdt': pad_lanes(wdt).astype(BF16),
        'wq': wq.astype(BF16), 'wk': wk.astype(BF16), 'wv': wv.astype(BF16),
        'q_scale': float(d_att // att_heads) ** -0.5,
        'conv_w': conv_w[i], 'conv_b': conv_b[i][None, :],
        'dt_bias': pad_lanes(dt_bias[i][None, :]), 'a_log': pad_lanes(a_log[i][None, :]),
        'dskip_e': jnp.repeat(d_skip[i], head_dim)[None, :], 'expand': expand, 'a_log_heads': heads,
        'ssd_norm_g': ssd_norm_g[i][None, :], 'attn_norm_g': attn_norm_g[i][None, :],
        'w_out': w_out[i].astype(BF16), 'ln1_g': ln1_g[i][None, :], 'ln1_b': ln1_b[i][None, :],
        'router_wt': jnp.pad(jnp.concatenate([_router_rows(piece) for piece in _split3(router_w.T)]),
                             ((0, LANES - 3 * ROUTER_ROWS), (0, 0))),
        'router_bias': _router_rows(router_bias[:, None]), 'n_experts': router_bias.shape[0],
        'w_gate': w_gate[i].astype(BF16), 'w_up': w_up[i].astype(BF16), 'w_down': w_down[i].astype(BF16),
        'ln2_g': ln2_g[i][None, :], 'ln2_b': ln2_b[i][None, :],
    }


def _trunk(x, layers, biases, conv_prev, h0, k_prev, v_prev, alpha, heads):
    b, t, d = x.shape
    n = b * t
    prompt = k_prev is None
    ks, vs, hs, cs = [], [], [], []
    x2 = x.reshape(n, d)
    tm = _pick(n, (512, 256, 128, 64, 32, 16, 8))
    tm_fin = _pick(n, (512, 256))
    n_buckets = N_EXPERT_GROUPS * PAIRS_PER_GROUP
    moe_tile = MOE_TILE if n >= 2 * n_buckets * MOE_TILE else MOE_TILE_SMALL
    assert n % moe_tile == 0
    for i, p in enumerate(layers):
        z, xbc, dt, q, k, v = _in_proj(x2, p, tm)
        d_conv = xbc.shape[1]
        d_ssm = z.shape[1]
        d_state = (d_conv - d_ssm) // (2 * SSM_GROUPS)
        xbc3 = xbc.reshape(b, t, d_conv)
        cprev8 = jnp.pad(conv_prev[i], ((0, 0), (SUBLANES - (CONV_W - 1), 0), (0, 0)))
        h0t = jnp.swapaxes(h0[i].reshape(b, SSM_GROUPS, d_ssm // SSM_GROUPS, d_state), 2, 3)
        blk = _pick(t, (512, 256, 128, 64, 32, 16, 8))
        y, ht = _ssd(xbc3, dt.reshape(b, t, LANES), cprev8, h0t, p, blk)
        q3, k3, v3 = (a.reshape(b, t, -1) for a in (q, k, v))
        if prompt:
            o = _attention(q3, k3, v3, None, None, biases[i], qb=BAND_PAST, sub=ATT_SUB,
                           mask_positions=True, heads=heads)
            keep = min(BAND_PAST, t)
            k_state, v_state = k3[:, t - keep:], v3[:, t - keep:]
        else:
            o = _attention(q3, k3, v3, k_prev[i], v_prev[i], biases[i], qb=t, sub=t,
                           mask_positions=False, heads=heads)
            k_state, v_state = k3, v3
        hx = _post_mixer(y.reshape(n, d_ssm), z, o.reshape(n, -1), x2, p, tm, alpha)
        bucket = hx[:, d + ROUTE_BUCKET].astype(jnp.int32)
        pos, src2d, elo, ehi, nused = _route_tables(bucket, N_EXPERT_GROUPS, moe_tile)
        y_sorted = _moe(hx, src2d, elo, ehi, nused, p, d)
        x2 = _finalize(y_sorted, pos.reshape(n // tm_fin, tm_fin), hx, p, d, alpha)
        head_dim = k3.shape[2] // heads
        ks.append(k_state.astype(F32).reshape(b, -1, heads, head_dim))
        vs.append(v_state.astype(F32).reshape(b, -1, heads, head_dim))
        hs.append(jnp.swapaxes(ht, 2, 3).reshape(h0[i].shape))
        full = jnp.concatenate([conv_prev[i], xbc3], axis=1) if t < CONV_W - 1 else xbc3
        cs.append(full[:, full.shape[1] - (CONV_W - 1):])
    return x2.reshape(b, t, d), jnp.stack(ks), jnp.stack(vs), jnp.stack(hs), jnp.stack(cs)


def kernel(x_prompt, x_sample, cache_k, cache_v, state_ssm, state_conv, w_in, conv_w, conv_b, dt_bias, a_log,
           d_skip, ssd_norm_g, attn_norm_g, rel_bias, w_out, ln1_g, ln1_b, router_w, router_bias, w_gate, w_up,
           w_down, ln2_g, ln2_b):
    depth = w_in.shape[0]
    heads = cache_k.shape[3]
    d_att = heads * cache_k.shape[4]
    d_conv = state_conv.shape[3]
    ssm_heads = state_ssm.shape[2]
    d_ssm = ssm_heads * state_ssm.shape[3]
    alpha = (2 * depth) ** 0.25
    dims = (d_ssm, d_conv, ssm_heads, d_att, heads)
    layers = [_layer_params(i, w_in, conv_w, conv_b, dt_bias, a_log, d_skip, ssd_norm_g, attn_norm_g, w_out,
                            ln1_g, ln1_b, router_w, router_bias, w_gate, w_up, w_down, ln2_g, ln2_b, dims)
              for i in range(depth)]
    bp, tp, _ = x_prompt.shape
    bs, ts, _ = x_sample.shape

    conv0 = jnp.zeros((depth, bp, CONV_W - 1, d_conv), F32)
    h00 = jnp.zeros((depth, bp) + state_ssm.shape[2:], F32)
    bias_p = [_band_bias(rel_bias[i], ATT_SUB, BAND_PAST + ATT_SUB, True) for i in range(depth)]
    y_p, k_p, v_p, h_p, c_p = _trunk(x_prompt, layers, bias_p, conv0, h00, None, None, alpha, heads)

    past = cache_k.shape[2]
    bias_s = [_band_bias(rel_bias[i], ts, past + ts, False) for i in range(depth)]
    ck = cache_k.reshape(depth, bs, past, d_att).astype(BF16)
    cv = cache_v.reshape(depth, bs, past, d_att).astype(BF16)
    y_s, k_s, v_s, h_s, c_s = _trunk(x_sample, layers, bias_s, state_conv, state_ssm, ck, cv, alpha, heads)
    return (y_p, y_s, k_p, v_p, h_p, c_p, k_s, v_s, h_s, c_s)
```
